```python
import jax, jax.numpy as jnp
from jax import lax
import numpy as np

D_MODEL = 2048
BATCH = 4
SEQ = 8192
DEPTH = 4

GRID_W = 64
CTX_LEN = 256
EPS = 1e-6
HEAD_DIM = 128
N_Q_HEADS = 12
N_KV_HEADS = 4
Q_PER_KV = N_Q_HEADS // N_KV_HEADS
ATTN_DIM = N_Q_HEADS * HEAD_DIM
KV_DIM = N_KV_HEADS * HEAD_DIM
ROPE_THETA = 10000.0
ROPE_AXIS_DIM = HEAD_DIM // 2
Q_BLOCK = 128
POOL_WINDOWS = (2, 4, 8, 16)
N_POOL_GROUPS = 4
POOL_DIM = D_MODEL // 4
POOL_GROUP_DIM = POOL_DIM // N_POOL_GROUPS
AP_IN_DIM = ATTN_DIM + 2 * KV_DIM + POOL_DIM
AP_MIX_DIM = ATTN_DIM + POOL_DIM
D_RNN = (4 * D_MODEL // 3) // 128 * 128
N_RNN_BLOCKS = 16
RNN_BLOCK_DIM = D_RNN // N_RNN_BLOCKS
CONV_WIDTH = 4
CONV_LEFT = 2
RG_C = 8.0
N_EXPERTS = 32
TOP_K = 4
D_EXPERT = 896
SWIGLU_LIMIT = 7.0
SWIGLU_ALPHA = 1.702
MOE_BLOCK = 128

kernel_name = "hybrid_gqa_pool_rglru_moe_prefix_dit"


def rms_norm(x, gain=None):
    x32 = x.astype(jnp.float32)
    y = x32 * lax.rsqrt(jnp.mean(x32 * x32, axis=-1, keepdims=True) + EPS)
    if gain is not None:
        y = y * gain.astype(jnp.float32)
    return y.astype(x.dtype)


def modulate(h, shift, scale):
    return h * (1 + scale) + shift


def rope_tables(row, col):
    inv_freq = ROPE_THETA ** (-jnp.arange(ROPE_AXIS_DIM // 2, dtype=jnp.float32) * 2.0 / ROPE_AXIS_DIM)
    ang_r = row.astype(jnp.float32)[:, None] * inv_freq
    ang_c = col.astype(jnp.float32)[:, None] * inv_freq
    return (jnp.cos(ang_r), jnp.sin(ang_r), jnp.cos(ang_c), jnp.sin(ang_c))


def apply_rope_2d(x, rope):
    cos_r, sin_r, cos_c, sin_c = rope
    x32 = x.astype(jnp.float32)

    def rot(v, cos, sin):
        half = v.shape[-1] // 2
        v1, v2 = v[..., :half], v[..., half:]
        cos = cos[None, :, None, :]
        sin = sin[None, :, None, :]
        return jnp.concatenate([v1 * cos - v2 * sin, v2 * cos + v1 * sin], axis=-1)

    y = jnp.concatenate([rot(x32[..., :ROPE_AXIS_DIM], cos_r, sin_r),
                         rot(x32[..., ROPE_AXIS_DIM:], cos_c, sin_c)], axis=-1)
    return y.astype(x.dtype)


def attend(q, k, v):
    b, lq = q.shape[:2]
    qg = q.reshape(b, lq, N_KV_HEADS, Q_PER_KV, HEAD_DIM)
    s = jnp.einsum('bqkgd,blkd->bkgql', qg, k, preferred_element_type=jnp.float32) * (HEAD_DIM ** -0.5)
    p = jax.nn.softmax(s, axis=-1).astype(v.dtype)
    o = jnp.einsum('bkgql,blkd->bqkgd', p, v)
    return o.reshape(b, lq, ATTN_DIM)


def block_attention(q, k, v):
    b, s = q.shape[:2]
    nb = s // Q_BLOCK
    qb = q.reshape(b, nb, Q_BLOCK, N_Q_HEADS, HEAD_DIM).swapaxes(0, 1)
    o = lax.map(lambda qi: attend(qi, k, v), qb)
    return o.swapaxes(0, 1).reshape(b, s, ATTN_DIM)


def multiscale_pool(z, pool_w, pool_scale):
    b, l, _ = z.shape
    zg = z.astype(jnp.float32).reshape(b, l, N_POOL_GROUPS, POOL_GROUP_DIM)
    cs = jnp.concatenate([jnp.zeros((b, 1, N_POOL_GROUPS, POOL_GROUP_DIM), jnp.float32),
                          jnp.cumsum(zg, axis=1)], axis=1)
    t = jnp.arange(l)
    outs = []
    for g, w in enumerate(POOL_WINDOWS):
        lo = jnp.clip(t - w // 2, 0, l)
        hi = jnp.clip(t + w - w // 2, 0, l)
        cnt = (hi - lo).astype(jnp.float32)[None, :, None]
        mean = (cs[:, hi, g] - cs[:, lo, g]) / cnt
        outs.append(mean - zg[:, :, g])
    d = jnp.stack(outs, axis=2).astype(pool_w.dtype)
    y = jnp.einsum('blgc,gcd->blgd', d, pool_w).reshape(b, l, POOL_DIM)
    return y * pool_scale


def attn_pool_mixer(hl, hc, w_in, q_gain, k_gain, pool_w, pool_scale, w_out, rope, need_ctx):
    b = hl.shape[0]

    def project(h):
        l = h.shape[1]
        z = h @ w_in
        q, k, v, p = jnp.split(z, [ATTN_DIM, ATTN_DIM + KV_DIM, ATTN_DIM + 2 * KV_DIM], axis=-1)
        q = rms_norm(q.reshape(b, l, N_Q_HEADS, HEAD_DIM), q_gain)
        k = rms_norm(k.reshape(b, l, N_KV_HEADS, HEAD_DIM), k_gain)
        v = v.reshape(b, l, N_KV_HEADS, HEAD_DIM)
        return q, k, v, p

    ql, kl, vl, pl = project(hl)
    qc, kc, vc, pc = project(hc)
    ql = apply_rope_2d(ql, rope)
    kl = apply_rope_2d(kl, rope)
    k_all = jnp.concatenate([kc, kl], axis=1)
    v_all = jnp.concatenate([vc, vl], axis=1)
    al = block_attention(ql, k_all, v_all)
    out_l = jnp.concatenate([al, multiscale_pool(pl, pool_w, pool_scale)], axis=-1) @ w_out
    out_c = None
    if need_ctx:
        ac = attend(qc, kc, vc)
        out_c = jnp.concatenate([ac, multiscale_pool(pc, pool_w, pool_scale)], axis=-1) @ w_out
    return out_l, out_c


def short_conv(u, w, bias):
    y = lax.conv_general_dilated(u, w[:, None, :], window_strides=(1,),
                                 padding=[(CONV_LEFT, CONV_WIDTH - 1 - CONV_LEFT)],
                                 dimension_numbers=('NWC', 'WIO', 'NWC'),
                                 feature_group_count=D_RNN)
    return y + bias


def block_diag(u, w):
    b, l, _ = u.shape
    ub = u.reshape(b, l, N_RNN_BLOCKS, RNN_BLOCK_DIM)
    return jnp.einsum('blnc,ncd->blnd', ub, w).reshape(b, l, D_RNN)


def linear_scan(a, bterm):
    def step(h, ab):
        a_t, b_t = ab
        h = a_t * h + b_t
        return h, h
    _, hs = lax.scan(step, jnp.zeros_like(a[:, 0]), (a.swapaxes(0, 1), bterm.swapaxes(0, 1)))
    return hs.swapaxes(0, 1)


def rglru_direction(uc, ul, r_w, r_b, i_w, i_b, lam, reverse):
    if reverse:
        uc, ul = uc[:, ::-1], ul[:, ::-1]
    n_ctx = uc.shape[1]
    u = jnp.concatenate([uc, ul], axis=1)
    r = jax.nn.sigmoid((block_diag(u, r_w) + r_b).astype(jnp.float32))
    i = jax.nn.sigmoid((block_diag(u, i_w) + i_b).astype(jnp.float32))
    log_a = -RG_C * r * jax.nn.softplus(-lam.astype(jnp.float32))
    a = jnp.exp(log_a)
    bterm = jnp.sqrt(jnp.maximum(-jnp.expm1(2.0 * log_a), 0.0)) * (i * u.astype(jnp.float32))
    h = linear_scan(a, bterm)
    hc, hl = h[:, :n_ctx], h[:, n_ctx:]
    if reverse:
        hc, hl = hc[:, ::-1], hl[:, ::-1]
    return hc, hl


def rglru_mixer(hl, hc, w_in, conv_w, conv_b, r_w, r_b, i_w, i_b, lam, w_out, need_ctx):
    gl, ul = jnp.split(hl @ w_in, [D_RNN], axis=-1)
    gc, uc = jnp.split(hc @ w_in, [D_RNN], axis=-1)
    ul = short_conv(ul, conv_w, conv_b)
    uc = short_conv(uc, conv_w, conv_b)
    hc_f, hl_f = rglru_direction(uc, ul, r_w[0], r_b[0], i_w[0], i_b[0], lam[0], False)
    hc_b, hl_b = rglru_direction(uc, ul, r_w[1], r_b[1], i_w[1], i_b[1], lam[1], True)
    out_l = (jax.nn.gelu(gl) * (hl_f + hl_b).astype(gl.dtype)) @ w_out
    out_c = None
    if need_ctx:
        out_c = (jax.nn.gelu(gc) * (hc_f + hc_b).astype(gc.dtype)) @ w_out
    return out_l, out_c


def moe_ffn(h, router_w, router_b, w_gu, b_gu, w_down, b_down):
    n, d = h.shape
    logits = (h @ router_w + router_b).astype(jnp.float32)
    top_logit, top_e = lax.top_k(logits, TOP_K)
    gate = jax.nn.softmax(top_logit, axis=-1)
    m = n * TOP_K
    flat_e = top_e.reshape(m).astype(jnp.int32)
    flat_tok = jnp.repeat(jnp.arange(n, dtype=jnp.int32), TOP_K)
    flat_g = gate.reshape(m)
    order = jnp.argsort(flat_e)
    e_s, tok_s, g_s = flat_e[order], flat_tok[order], flat_g[order]
    counts = jnp.bincount(flat_e, length=N_EXPERTS).astype(jnp.int32)
    padded = (counts + MOE_BLOCK - 1) // MOE_BLOCK * MOE_BLOCK
    start = jnp.cumsum(counts) - counts
    pend = jnp.cumsum(padded)
    pstart = pend - padded
    dest = pstart[e_s] + jnp.arange(m, dtype=jnp.int32) - start[e_s]
    n_blocks = -(-m // MOE_BLOCK) + N_EXPERTS
    p_rows = n_blocks * MOE_BLOCK
    row_tok = jnp.full((p_rows,), n, jnp.int32).at[dest].set(tok_s)
    row_g = jnp.zeros((p_rows,), jnp.float32).at[dest].set(g_s)
    blk_e = jnp.minimum(jnp.searchsorted(pend, jnp.arange(n_blocks, dtype=jnp.int32) * MOE_BLOCK,
                                         side='right'), N_EXPERTS - 1)
    h_pad = jnp.concatenate([h, jnp.zeros((1, d), h.dtype)], axis=0)

    def expert_block(args):
        rows, e = args
        xb = h_pad[rows]
        gu = xb @ w_gu[e] + b_gu[e]
        g, u = gu[:, 0::2], gu[:, 1::2]
        g = jnp.minimum(g, SWIGLU_LIMIT)
        u = jnp.clip(u, -SWIGLU_LIMIT, SWIGLU_LIMIT)
        act = (u + 1) * (g * jax.nn.sigmoid(SWIGLU_ALPHA * g))
        return act @ w_down[e] + b_down[e]

    y = lax.map(expert_block, (row_tok.reshape(n_blocks, MOE_BLOCK), blk_e))
    y = y.reshape(p_rows, d).astype(jnp.float32) * row_g[:, None]
    return jax.ops.segment_sum(y, row_tok, num_segments=n + 1)[:n].astype(h.dtype)


def setup_inputs(seed: int = 0) -> dict:
    key = jax.random.key(seed)
    ks = jax.random.split(key, 32)
    n_even = (DEPTH + 1) // 2
    n_odd = DEPTH // 2
    d = D_MODEL

    def nrm(k, shape, scale):
        return jax.random.normal(k, shape, jnp.float32) * scale

    u = jax.random.uniform(ks[19], (n_odd, 2, D_RNN), jnp.float32, 0.9, 0.999)
    return {
        "x": nrm(ks[0], (BATCH, SEQ, d), 1.0),
        "c": nrm(ks[1], (BATCH, d), 1.0),
        "ctx": nrm(ks[2], (BATCH, CTX_LEN, d), 1.0),
        "c_ctx": nrm(ks[3], (d,), 1.0),
        "mod_w": nrm(ks[4], (DEPTH, d, 6 * d), 0.5 * d ** -0.5),
        "mod_b": nrm(ks[5], (DEPTH, 6 * d), 0.02),
        "ap_w_in": nrm(ks[6], (n_even, d, AP_IN_DIM), d ** -0.5),
        "ap_q_gain": 1.0 + nrm(ks[7], (n_even, HEAD_DIM), 0.02),
        "ap_k_gain": 1.0 + nrm(ks[8], (n_even, HEAD_DIM), 0.02),
        "ap_pool_w": nrm(ks[9], (n_even, N_POOL_GROUPS, POOL_GROUP_DIM, POOL_GROUP_DIM), POOL_GROUP_DIM ** -0.5),
        "ap_pool_scale": 1.0 + nrm(ks[10], (n_even, POOL_DIM), 0.02),
        "ap_w_out": nrm(ks[11], (n_even, AP_MIX_DIM, d), AP_MIX_DIM ** -0.5),
        "rg_w_in": nrm(ks[12], (n_odd, d, 2 * D_RNN), d ** -0.5),
        "rg_conv_w": nrm(ks[13], (n_odd, CONV_WIDTH, D_RNN), CONV_WIDTH ** -0.5),
        "rg_conv_b": nrm(ks[14], (n_odd, D_RNN), 0.02),
        "rg_gate_r_w": nrm(ks[15], (n_odd, 2, N_RNN_BLOCKS, RNN_BLOCK_DIM, RNN_BLOCK_DIM), RNN_BLOCK_DIM ** -0.5),
        "rg_gate_r_b": nrm(ks[16], (n_odd, 2, D_RNN), 0.02),
        "rg_gate_i_w": nrm(ks[17], (n_odd, 2, N_RNN_BLOCKS, RNN_BLOCK_DIM, RNN_BLOCK_DIM), RNN_BLOCK_DIM ** -0.5),
        "rg_gate_i_b": nrm(ks[18], (n_odd, 2, D_RNN), 0.02),
        "rg_lambda": jnp.log(u) - jnp.log1p(-u),
        "rg_w_out": nrm(ks[20], (n_odd, D_RNN, d), D_RNN ** -0.5),
        "moe_router_w": nrm(ks[21], (DEPTH, d, N_EXPERTS), d ** -0.5),
        "moe_router_b": nrm(ks[22], (DEPTH, N_EXPERTS), 0.01),
        "moe_w_gu": nrm(ks[23], (DEPTH, N_EXPERTS, d, 2 * D_EXPERT), d ** -0.5),
        "moe_b_gu": nrm(ks[24], (DEPTH, N_EXPERTS, 2 * D_EXPERT), 0.02),
        "moe_w_down": nrm(ks[25], (DEPTH, N_EXPERTS, D_EXPERT, d), D_EXPERT ** -0.5),
        "moe_b_down": nrm(ks[26], (DEPTH, N_EXPERTS, d), 0.02),
        "final_norm_g": 1.0 + nrm(ks[27], (d,), 0.02),
    }


def reference(x, c, ctx, c_ctx, mod_w, mod_b, ap_w_in, ap_q_gain, ap_k_gain, ap_pool_w, ap_pool_scale,
              ap_w_out, rg_w_in, rg_conv_w, rg_conv_b, rg_gate_r_w, rg_gate_r_b, rg_gate_i_w, rg_gate_i_b,
              rg_lambda, rg_w_out, moe_router_w, moe_router_b, moe_w_gu, moe_b_gu, moe_w_down, moe_b_down,
              final_norm_g):
    b, s, d = x.shape
    n_ctx = ctx.shape[1]
    ROWS = s // GRID_W
    row = jnp.repeat(jnp.arange(ROWS, dtype=jnp.int32), GRID_W)
    col = jnp.tile(jnp.arange(GRID_W, dtype=jnp.int32), ROWS)
    rope = rope_tables(row, col)
    silu_c = jax.nn.silu(c)
    silu_cc = jax.nn.silu(c_ctx)
    xl, xc = x, ctx
    for l in range(DEPTH):
        last = l == DEPTH - 1
        mods_l = jnp.split(silu_c @ mod_w[l] + mod_b[l], 6, axis=-1)
        sh_l, sc_l, g_l, shf_l, scf_l, gf_l = [m[:, None, :] for m in mods_l]
        sh_c, sc_c, g_c, shf_c, scf_c, gf_c = jnp.split(silu_cc @ mod_w[l] + mod_b[l], 6, axis=-1)
        hl = modulate(rms_norm(xl), sh_l, sc_l)
        hc = modulate(rms_norm(xc), sh_c, sc_c)
        i = l // 2
        if l % 2 == 0:
            ol, oc = attn_pool_mixer(hl, hc, ap_w_in[i], ap_q_gain[i], ap_k_gain[i], ap_pool_w[i],
                                     ap_pool_scale[i], ap_w_out[i], rope, not last)
        else:
            ol, oc = rglru_mixer(hl, hc, rg_w_in[i], rg_conv_w[i], rg_conv_b[i], rg_gate_r_w[i],
                                 rg_gate_r_b[i], rg_gate_i_w[i], rg_gate_i_b[i], rg_lambda[i],
                                 rg_w_out[i], not last)
        xl = xl + g_l * ol
        hl = modulate(rms_norm(xl), shf_l, scf_l)
        if last:
            yl = moe_ffn(hl.reshape(b * s, d), moe_router_w[l], moe_router_b[l], moe_w_gu[l],
                         moe_b_gu[l], moe_w_down[l], moe_b_down[l])
            xl = xl + gf_l * yl.reshape(b, s, d)
        else:
            xc = xc + g_c * oc
            hc = modulate(rms_norm(xc), shf_c, scf_c)
            tokens = jnp.concatenate([hl.reshape(b * s, d), hc.reshape(b * n_ctx, d)], axis=0)
            y = moe_ffn(tokens, moe_router_w[l], moe_router_b[l], moe_w_gu[l], moe_b_gu[l],
                        moe_w_down[l], moe_b_down[l])
            xl = xl + gf_l * y[:b * s].reshape(b, s, d)
            xc = xc + gf_c * y[b * s:].reshape(b, n_ctx, d)
    return rms_norm(xl, final_norm_g)
```

```python
import functools

import jax
import jax.numpy as jnp
import numpy as np
from jax import lax
from jax.experimental import pallas as pl
from jax.experimental.pallas import tpu as pltpu

F32 = jnp.float32
BF16 = jnp.bfloat16
U32 = jnp.uint32

EPS = 1e-6
GRID_W = 64
HEAD_DIM = 128
N_Q_HEADS = 12
N_KV_HEADS = 4
Q_PER_KV = N_Q_HEADS // N_KV_HEADS
ATTN_DIM = N_Q_HEADS * HEAD_DIM
KV_DIM = N_KV_HEADS * HEAD_DIM
ROPE_THETA = 10000.0
ROPE_AXIS_DIM = HEAD_DIM // 2
POOL_WINDOWS = (2, 4, 8, 16)
N_POOL_GROUPS = 4
N_RNN_BLOCKS = 16
CONV_WIDTH = 4
CONV_LEFT = 2
RG_C = 8.0
TOP_K = 4
SWIGLU_LIMIT = 7.0
SWIGLU_ALPHA = 1.702

LANES = 128
SUBLANES = 8
MXU_DIM = 256
VMEM_LIMIT = 56 * 1024 * 1024

TM = 256
HALO = SUBLANES
ATT_CK = 512
MOE_BLK = 256
GATE_TN = 256
SCAN_LANES = 896


def _cparams(sem):
    return pltpu.CompilerParams(dimension_semantics=sem, vmem_limit_bytes=VMEM_LIMIT)


def _sigmoid(x):
    return 1.0 / (1.0 + jnp.exp(-x))


def _norm_mod(x, mods, i_shift, i_scale):
    ms = jnp.mean(x * x, axis=-1, keepdims=True)
    xn = x * lax.rsqrt(ms + EPS)
    return xn * (1.0 + mods[i_scale:i_scale + 1, :]) + mods[i_shift:i_shift + 1, :]


def _pack_bf16_pair(h):
    c = h.shape[-1] // 2
    bits = pltpu.bitcast(h.astype(BF16).astype(F32), U32)
    return (bits[:, :c] & jnp.uint32(0xFFFF0000)) | (bits[:, c:] >> jnp.uint32(16))


def _unpack_hi(w):
    return pltpu.bitcast(w & jnp.uint32(0xFFFF0000), F32)


def _unpack_lo(w):
    return pltpu.bitcast(w << jnp.uint32(16), F32)


def _mods_kernel(cv_ref, w_ref, b_ref, o_ref):
    cv = cv_ref[...]
    s = cv * _sigmoid(cv)
    o_ref[0] = jnp.dot(s.astype(BF16), w_ref[0].astype(BF16), preferred_element_type=F32) + b_ref[0]


def _mods_call(cv, mod_w, mod_b):
    depth, d, n6 = mod_w.shape
    tn = 1536
    return pl.pallas_call(
        _mods_kernel,
        grid=(depth, n6 // tn),
        in_specs=[pl.BlockSpec((SUBLANES, d), lambda l, j: (0, 0)),
                  pl.BlockSpec((1, d, tn), lambda l, j: (l, 0, j)),
                  pl.BlockSpec((1, 1, tn), lambda l, j: (l, 0, j))],
        out_specs=pl.BlockSpec((1, SUBLANES, tn), lambda l, j: (l, 0, j)),
        out_shape=jax.ShapeDtypeStruct((depth, SUBLANES, n6), F32),
        compiler_params=_cparams(("parallel", "parallel")),
        name="adaln_mods",
    )(cv, mod_w, mod_b.reshape(depth, 1, n6))


def _mods_spec(d):
    return pl.BlockSpec((1, 1, 6, d), lambda b, t: (jnp.minimum(t, 1), b, 0, 0))


def _attn_in_kernel(x_ref, mods_ref, w_ref, qg_ref, kg_ref, cos_ref, sin_ref,
                    q_ref, k_ref, v_ref, p_ref):
    h = _norm_mod(x_ref[0], mods_ref[0, 0], 0, 1).astype(BF16)
    z = jnp.dot(h, w_ref[...], preferred_element_type=F32)
    cos = cos_ref[...]
    sin = sin_ref[...]
    lane = lax.broadcasted_iota(jnp.int32, (TM, HEAD_DIM), 1)
    first = (lane % (ROPE_AXIS_DIM)) < (ROPE_AXIS_DIM // 2)

    def head(zh, gain, scale):
        n = zh * lax.rsqrt(jnp.mean(zh * zh, axis=-1, keepdims=True) + EPS) * gain
        half = ROPE_AXIS_DIM // 2
        swapped = jnp.where(first, pltpu.roll(n, HEAD_DIM - half, 1), pltpu.roll(n, half, 1))
        return (n * cos + swapped * sin) * scale

    qg = qg_ref[...]
    kg = kg_ref[...]
    for hh in range(N_Q_HEADS):
        sl = slice(hh * HEAD_DIM, (hh + 1) * HEAD_DIM)
        q_ref[0, :, sl] = head(z[:, sl], qg, HEAD_DIM ** -0.5).astype(BF16)
    for hh in range(N_KV_HEADS):
        sl = slice(hh * HEAD_DIM, (hh + 1) * HEAD_DIM)
        zs = slice(ATTN_DIM + hh * HEAD_DIM, ATTN_DIM + (hh + 1) * HEAD_DIM)
        k_ref[0, :, sl] = head(z[:, zs], kg, 1.0).astype(BF16)
    v_ref[0] = z[:, ATTN_DIM + KV_DIM:ATTN_DIM + 2 * KV_DIM].astype(BF16)
    p_ref[0] = z[:, ATTN_DIM + 2 * KV_DIM:]


def _attn_in_call(x, mods, w_in, q_gain, k_gain, cos_t, sin_t):
    b, l, d = x.shape
    t = l // TM
    n_in = w_in.shape[1]
    pool_dim = n_in - ATTN_DIM - 2 * KV_DIM
    row = lambda bb, tt: (bb, tt, 0)
    return pl.pallas_call(
        _attn_in_kernel,
        grid=(b, t),
        in_specs=[pl.BlockSpec((1, TM, d), row),
                  _mods_spec(d),
                  pl.BlockSpec((d, n_in), lambda bb, tt: (0, 0)),
                  pl.BlockSpec((1, HEAD_DIM), lambda bb, tt: (0, 0)),
                  pl.BlockSpec((1, HEAD_DIM), lambda bb, tt: (0, 0)),
                  pl.BlockSpec((TM, HEAD_DIM), lambda bb, tt: (tt, 0)),
                  pl.BlockSpec((TM, HEAD_DIM), lambda bb, tt: (tt, 0))],
        out_specs=[pl.BlockSpec((1, TM, ATTN_DIM), row),
                   pl.BlockSpec((1, TM, KV_DIM), row),
                   pl.BlockSpec((1, TM, KV_DIM), row),
                   pl.BlockSpec((1, TM, pool_dim), row)],
        out_shape=[jax.ShapeDtypeStruct((b, l, ATTN_DIM), BF16),
                   jax.ShapeDtypeStruct((b, l, KV_DIM), BF16),
                   jax.ShapeDtypeStruct((b, l, KV_DIM), BF16),
                   jax.ShapeDtypeStruct((b, l, pool_dim), F32)],
        compiler_params=_cparams(("parallel", "parallel")),
        name="attn_in",
    )(x, mods, w_in, q_gain, k_gain, cos_t, sin_t)


def _attention_kernel(q_ref, k_ref, v_ref, o_ref, m_sc, l_sc, acc_sc, *, n_ctx, n_lat_chunks):
    t = pl.program_id(2)
    q = jnp.concatenate([q_ref[0, :, g * HEAD_DIM:(g + 1) * HEAD_DIM] for g in range(Q_PER_KV)],
                        axis=0)
    m_sc[...] = jnp.full(m_sc.shape, -1e30, F32)
    l_sc[...] = jnp.zeros(l_sc.shape, F32)
    acc_sc[...] = jnp.zeros(acc_sc.shape, F32)

    def chunk(start, size):
        k = k_ref[0, pl.ds(start, size), :]
        v = v_ref[0, pl.ds(start, size), :]
        s = lax.dot_general(q, k, (((1,), (1,)), ((), ())), preferred_element_type=F32)
        m_prev = m_sc[...]
        m_new = jnp.maximum(m_prev, jnp.max(s, axis=-1, keepdims=True))
        p = jnp.exp(s - m_new)
        alpha = jnp.exp(m_prev - m_new)
        l_sc[...] = alpha * l_sc[...] + jnp.sum(p, axis=-1, keepdims=True)
        acc_sc[...] = alpha * acc_sc[...] + jnp.dot(p.astype(BF16), v, preferred_element_type=F32)
        m_sc[...] = m_new

    chunk(0, n_ctx)

    @pl.when(t > 0)
    def _():
        def body(j, carry):
            chunk(pl.multiple_of(n_ctx + j * ATT_CK, ATT_CK // 2), ATT_CK)
            return carry
        lax.fori_loop(0, n_lat_chunks, body, 0)

    o = acc_sc[...] / l_sc[...]
    for g in range(Q_PER_KV):
        o_ref[0, :, g * HEAD_DIM:(g + 1) * HEAD_DIM] = o[g * TM:(g + 1) * TM].astype(BF16)


def _attention_call(q, k, v, n_ctx):
    b, l, _ = q.shape
    t = l // TM
    gw = Q_PER_KV * HEAD_DIM
    kern = functools.partial(_attention_kernel, n_ctx=n_ctx, n_lat_chunks=(l - n_ctx) // ATT_CK)
    return pl.pallas_call(
        kern,
        grid=(b, N_KV_HEADS, t),
        in_specs=[pl.BlockSpec((1, TM, gw), lambda bb, hh, tt: (bb, tt, hh)),
                  pl.BlockSpec((1, l, HEAD_DIM), lambda bb, hh, tt: (bb, 0, hh)),
                  pl.BlockSpec((1, l, HEAD_DIM), lambda bb, hh, tt: (bb, 0, hh))],
        out_specs=pl.BlockSpec((1, TM, gw), lambda bb, hh, tt: (bb, tt, hh)),
        out_shape=jax.ShapeDtypeStruct((b, l, ATTN_DIM), BF16),
        scratch_shapes=[pltpu.VMEM((Q_PER_KV * TM, 1), F32),
                        pltpu.VMEM((Q_PER_KV * TM, 1), F32),
                        pltpu.VMEM((Q_PER_KV * TM, HEAD_DIM), F32)],
        compiler_params=_cparams(("parallel", "parallel", "parallel")),
        name="attention",
    )(q, k, v)


def _residual_and_route(x, mix, mods, rw_ref, rb_ref, cnt_sc, xo_ref, hp_ref, meta_ref):
    first_step = (pl.program_id(0) == 0) & (pl.program_id(1) == 0)

    @pl.when(first_step)
    def _():
        cnt_sc[...] = jnp.zeros(cnt_sc.shape, F32)

    xnew = x + mods[2:3, :] * mix
    xo_ref[0] = xnew
    h2 = _norm_mod(xnew, mods, 3, 4)
    hp_ref[...] = _pack_bf16_pair(h2)

    n_exp = rw_ref.shape[1]
    logits = jnp.dot(h2, rw_ref[...], preferred_element_type=F32,
                     precision=lax.Precision.HIGHEST) + rb_ref[...]
    lane = lax.broadcasted_iota(jnp.int32, (TM, n_exp), 1).astype(F32)
    onehots, tops, idxs = [], [], []
    cur = logits
    for _ in range(TOP_K):
        m = jnp.max(cur, axis=-1, keepdims=True)
        idx = jnp.min(jnp.where(cur == m, lane, float(n_exp)), axis=-1, keepdims=True)
        sel = lane == idx
        tops.append(m)
        idxs.append(idx)
        onehots.append(sel)
        cur = jnp.where(sel, -jnp.inf, cur)
    ws = [jnp.exp(m - tops[0]) for m in tops]
    denom = ws[0] + ws[1] + ws[2] + ws[3]

    counts = jnp.zeros((TM, n_exp), F32)
    for sel in onehots:
        counts = counts + jnp.where(sel, 1.0, 0.0)
    ri = lax.broadcasted_iota(jnp.int32, (TM, TM), 0)
    ci = lax.broadcasted_iota(jnp.int32, (TM, TM), 1)
    tri = jnp.where(ci < ri, 1.0, 0.0).astype(BF16)
    before = jnp.dot(tri, counts.astype(BF16), preferred_element_type=F32) + cnt_sc[...]

    lane_m = lax.broadcasted_iota(jnp.int32, (TM, LANES), 1)
    meta = jnp.zeros((TM, LANES), F32)
    for kk in range(TOP_K):
        rank = jnp.sum(jnp.where(onehots[kk], before, 0.0), axis=-1, keepdims=True)
        meta = jnp.where(lane_m == kk, idxs[kk], meta)
        meta = jnp.where(lane_m == TOP_K + kk, ws[kk] / denom, meta)
        meta = jnp.where(lane_m == 2 * TOP_K + kk, rank, meta)
    meta_ref[...] = meta
    cnt_sc[...] = cnt_sc[...] + jnp.sum(counts, axis=0, keepdims=True)
    return cnt_sc[...]


def _route_out_specs(t, d, n_exp):
    flat = lambda bb, tt: (bb * t + tt, 0)
    return [pl.BlockSpec((1, TM, d), lambda bb, tt: (bb, tt, 0)),
            pl.BlockSpec((TM, d // 2), flat),
            pl.BlockSpec((TM, LANES), flat),
            pl.BlockSpec((1, n_exp), lambda bb, tt: (0, 0))]


def _route_out_shapes(b, l, d, n_exp):
    return [jax.ShapeDtypeStruct((b, l, d), F32),
            jax.ShapeDtypeStruct((b * l, d // 2), U32),
            jax.ShapeDtypeStruct((b * l, LANES), F32),
            jax.ShapeDtypeStruct((1, n_exp), F32)]


def _attn_out_kernel(x_ref, mods_ref, ao_ref, p_ref, pp_ref, pn_ref, pw_ref, ps_ref,
                     woa_ref, wop_ref, rw_ref, rb_ref,
                     xo_ref, hp_ref, meta_ref, cnt_ref, e_sc, cnt_sc, *, n_ctx, n_lat):
    t = pl.program_id(1)
    nt = pl.num_programs(1)
    prev_ok = t >= 2
    next_ok = (t >= 1) & (t <= nt - 2)
    e_sc[0:HALO, :] = jnp.where(prev_ok, pp_ref[0], 0.0)
    e_sc[HALO:HALO + TM, :] = p_ref[0]
    e_sc[HALO + TM:, :] = jnp.where(next_ok, pn_ref[0], 0.0)

    rowi = lax.broadcasted_iota(jnp.int32, (TM, 1), 0)
    pos = jnp.where(t == 0, rowi, (t - 1) * TM + rowi)
    seq_len = jnp.where(t == 0, n_ctx, n_lat)
    gw = p_ref.shape[-1] // N_POOL_GROUPS
    mix = jnp.dot(ao_ref[0], woa_ref[...], preferred_element_type=F32)
    for g, w in enumerate(POOL_WINDOWS):
        ls = slice(g * gw, (g + 1) * gw)
        acc = e_sc[HALO - w // 2:HALO - w // 2 + TM, ls]
        for j in range(1, w):
            off = HALO - w // 2 + j
            acc = acc + e_sc[off:off + TM, ls]
        lo = jnp.maximum(pos - w // 2, 0)
        hi = jnp.minimum(pos + w - w // 2, seq_len)
        dlt = acc / (hi - lo).astype(F32) - e_sc[HALO:HALO + TM, ls]
        y = jnp.dot(dlt.astype(BF16), pw_ref[g], preferred_element_type=F32) * ps_ref[:, ls]
        mix = mix + jnp.dot(y.astype(BF16), wop_ref[ls, :], preferred_element_type=F32)

    cnt_ref[...] = _residual_and_route(x_ref[0], mix, mods_ref[0, 0], rw_ref, rb_ref, cnt_sc,
                                       xo_ref, hp_ref, meta_ref)


def _attn_out_call(x, mods, ao, p, pool_w, pool_scale, wo_a, wo_p, rw, rb, n_ctx):
    b, l, d = x.shape
    t = l // TM
    pool_dim = p.shape[-1]
    n_exp = rw.shape[1]
    nh = l // HALO
    per = TM // HALO
    row = lambda bb, tt: (bb, tt, 0)
    const2 = lambda bb, tt: (0, 0)
    kern = functools.partial(_attn_out_kernel, n_ctx=n_ctx, n_lat=l - n_ctx)
    return pl.pallas_call(
        kern,
        grid=(b, t),
        in_specs=[pl.BlockSpec((1, TM, d), row),
                  _mods_spec(d),
                  pl.BlockSpec((1, TM, ATTN_DIM), row),
                  pl.BlockSpec((1, TM, pool_dim), row),
                  pl.BlockSpec((1, HALO, pool_dim), lambda bb, tt: (bb, jnp.maximum(tt * per - 1, 0), 0)),
                  pl.BlockSpec((1, HALO, pool_dim), lambda bb, tt: (bb, jnp.minimum((tt + 1) * per, nh - 1), 0)),
                  pl.BlockSpec(pool_w.shape, lambda bb, tt: (0, 0, 0)),
                  pl.BlockSpec((1, pool_dim), const2),
                  pl.BlockSpec(wo_a.shape, const2),
                  pl.BlockSpec(wo_p.shape, const2),
                  pl.BlockSpec(rw.shape, const2),
                  pl.BlockSpec((1, n_exp), const2)],
        out_specs=_route_out_specs(t, d, n_exp),
        out_shape=_route_out_shapes(b, l, d, n_exp),
        scratch_shapes=[pltpu.VMEM((TM + 2 * HALO, pool_dim), F32),
                        pltpu.VMEM((1, n_exp), F32)],
        compiler_params=_cparams(("arbitrary", "arbitrary")),
        name="attn_out",
    )(x, mods, ao, p, p, p, pool_w, pool_scale, wo_a, wo_p, rw, rb)


def _gelu_tanh(x):
    return 0.5 * x * (1.0 + jnp.tanh(0.7978845608028654 * (x + 0.044715 * (x * x * x))))


def _rg_in_kernel(x_ref, mods_ref, w_ref, o_ref, *, gelu):
    h = _norm_mod(x_ref[0], mods_ref[0, 0], 0, 1).astype(BF16)
    z = jnp.dot(h, w_ref[...], preferred_element_type=F32)
    if gelu:
        z = _gelu_tanh(z)
    o_ref[0] = z.astype(o_ref.dtype)


def _rg_in_call(x, mods, w, gelu, out_dtype, name):
    b, l, d = x.shape
    t = l // TM
    n = w.shape[1]
    row = lambda bb, tt: (bb, tt, 0)
    return pl.pallas_call(
        functools.partial(_rg_in_kernel, gelu=gelu),
        grid=(b, t),
        in_specs=[pl.BlockSpec((1, TM, d), row), _mods_spec(d),
                  pl.BlockSpec((d, n), lambda bb, tt: (0, 0))],
        out_specs=pl.BlockSpec((1, TM, n), row),
        out_shape=jax.ShapeDtypeStruct((b, l, n), out_dtype),
        compiler_params=_cparams(("parallel", "parallel")),
        name=name,
    )(x, mods, w)


def _gate_slab_starts(d_rnn, blk):
    n_slabs = -(-d_rnn // GATE_TN)
    spans = []
    for j in range(n_slabs):
        c0, c1 = j * GATE_TN, min((j + 1) * GATE_TN, d_rnn)
        lo = (c0 // blk) * blk
        hi = ((c1 - 1) // blk + 1) * blk
        spans.append((lo // LANES * LANES, -(-hi // LANES) * LANES))
    km = max(h - l for l, h in spans)
    starts = [min(l, d_rnn - km) for l, _ in spans]
    return starts, km


def _gate_slabs(r_w, i_w, d_rnn):
    blk = r_w.shape[-1]
    starts, km = _gate_slab_starts(d_rnn, blk)
    n_slabs = len(starts)
    pad = n_slabs * GATE_TN - d_rnn
    dense = [jnp.pad(jax.scipy.linalg.block_diag(*[w[n] for n in range(w.shape[0])]),
                     ((0, 0), (0, pad))) for w in (r_w, i_w)]
    slabs = [jnp.concatenate([dd[s:s + km, j * GATE_TN:(j + 1) * GATE_TN] for dd in dense], axis=1)
             for j, s in enumerate(starts)]
    return jnp.stack(slabs).astype(BF16)


def _rg_scan_kernel(u_ref, up_ref, un_ref, cw_ref, cb_ref, w_ref, gb_ref, lam_ref, h_ref,
                    e_sc, ucb_sc, a_sc, b_sc, carry_sc, *, reverse, starts, km):
    s = pl.program_id(1)
    nt = pl.num_programs(1)
    tau = jnp.where(s == 0, 0, nt - s) if reverse else s
    d_rnn = u_ref.shape[-1]

    @pl.when(s == 0)
    def _():
        carry_sc[...] = jnp.zeros(carry_sc.shape, F32)

    prev_ok = tau >= 2
    next_ok = (tau >= 1) & (tau <= nt - 2)
    e_sc[0:HALO, :] = jnp.where(prev_ok, up_ref[0], 0.0)
    e_sc[HALO:HALO + TM, :] = u_ref[0]
    e_sc[HALO + TM:, :] = jnp.where(next_ok, un_ref[0], 0.0)
    cw = cw_ref[...]
    n_conv_chunks = d_rnn // (3 * LANES)
    for c in range(n_conv_chunks):
        ls = slice(c * 3 * LANES, (c + 1) * 3 * LANES)
        uc = cb_ref[:, ls] + cw[0:1, ls] * e_sc[HALO - CONV_LEFT:HALO - CONV_LEFT + TM, ls]
        for j in range(1, CONV_WIDTH):
            off = HALO - CONV_LEFT + j
            uc = uc + cw[j:j + 1, ls] * e_sc[off:off + TM, ls]
        a_sc[:, ls] = uc
        ucb_sc[:, ls] = uc.astype(BF16)

    lam = lam_ref[0]
    neg = -lam
    softplus = jnp.maximum(neg, 0.0) + jnp.log(1.0 + jnp.exp(-jnp.abs(neg)))
    gb = gb_ref[0]
    for j, st in enumerate(starts):
        n = min(GATE_TN, d_rnn - j * GATE_TN)
        ls = slice(j * GATE_TN, j * GATE_TN + n)
        z = jnp.dot(ucb_sc[:, st:st + km], w_ref[0, j], preferred_element_type=F32)
        r = _sigmoid(z[:, :n] + gb[0:1, ls])
        i = _sigmoid(z[:, GATE_TN:GATE_TN + n] + gb[1:2, ls])
        log_a = (-RG_C) * r * softplus[:, ls]
        a = jnp.exp(log_a)
        mult = jnp.sqrt(jnp.maximum(1.0 - a * a, 0.0))
        b_sc[:, ls] = mult * (i * a_sc[:, ls])
        a_sc[:, ls] = a

    n_groups = TM // SUBLANES
    rowi = lax.broadcasted_iota(jnp.int32, (SUBLANES, SCAN_LANES), 0)
    for c in range(d_rnn // SCAN_LANES):
        ls = slice(c * SCAN_LANES, (c + 1) * SCAN_LANES)

        def body(gi, h):
            g = (n_groups - 1 - gi) if reverse else gi
            rows = pl.ds(pl.multiple_of(g * SUBLANES, SUBLANES), SUBLANES)
            a = a_sc[rows, ls]
            b = b_sc[rows, ls]
            for sft in (1, 2, 4):
                if reverse:
                    ok = rowi < SUBLANES - sft
                    shift = SUBLANES - sft
                else:
                    ok = rowi >= sft
                    shift = sft
                ap = jnp.where(ok, pltpu.roll(a, shift, 0), 1.0)
                bp = jnp.where(ok, pltpu.roll(b, shift, 0), 0.0)
                b = a * bp + b
                a = a * ap
            hh = a * h + b
            b_sc[rows, ls] = hh
            last = hh[0:1, :] if reverse else hh[SUBLANES - 1:SUBLANES, :]
            return jnp.broadcast_to(last, (SUBLANES, SCAN_LANES))

        carry_sc[:, ls] = lax.fori_loop(0, n_groups, body, carry_sc[:, ls])

    h_ref[0] = b_sc[...].astype(h_ref.dtype)


def _rg_scan_call(u, conv_w, conv_b, slabs, gate_b, lam, reverse, blk):
    b, l, d_rnn = u.shape
    t = l // TM
    nh = l // HALO
    per = TM // HALO
    starts, km = _gate_slab_starts(d_rnn, blk)

    def tile(ss):
        return jnp.where(ss == 0, 0, t - ss) if reverse else ss

    kern = functools.partial(_rg_scan_kernel, reverse=reverse, starts=tuple(starts), km=km)
    const2 = lambda bb, ss: (0, 0)
    return pl.pallas_call(
        kern,
        grid=(b, t),
        in_specs=[pl.BlockSpec((1, TM, d_rnn), lambda bb, ss: (bb, tile(ss), 0)),
                  pl.BlockSpec((1, HALO, d_rnn),
                               lambda bb, ss: (bb, jnp.maximum(tile(ss) * per - 1, 0), 0)),
                  pl.BlockSpec((1, HALO, d_rnn),
                               lambda bb, ss: (bb, jnp.minimum((tile(ss) + 1) * per, nh - 1), 0)),
                  pl.BlockSpec(conv_w.shape, const2),
                  pl.BlockSpec((1, d_rnn), const2),
                  pl.BlockSpec((1,) + slabs.shape, lambda bb, ss: (0, 0, 0, 0)),
                  pl.BlockSpec((1, 2, d_rnn), lambda bb, ss: (0, 0, 0)),
                  pl.BlockSpec((1, 1, d_rnn), lambda bb, ss: (0, 0, 0))],
        out_specs=pl.BlockSpec((1, TM, d_rnn), lambda bb, ss: (bb, tile(ss), 0)),
        out_shape=jax.ShapeDtypeStruct((b, l, d_rnn), BF16),
        scratch_shapes=[pltpu.VMEM((TM + 2 * HALO, d_rnn), F32),
                        pltpu.VMEM((TM, d_rnn), BF16),
                        pltpu.VMEM((TM, d_rnn), F32),
                        pltpu.VMEM((TM, d_rnn), F32),
                        pltpu.VMEM((SUBLANES, d_rnn), F32)],
        compiler_params=_cparams(("arbitrary", "arbitrary")),
        name="rg_scan_bwd" if reverse else "rg_scan_fwd",
    )(u, u, u, conv_w, conv_b, slabs[None], gate_b[None], lam[None, None])


def _rg_out_kernel(x_ref, mods_ref, g_ref, hf_ref, hb_ref, wo_ref, rw_ref, rb_ref,
                   xo_ref, hp_ref, meta_ref, cnt_ref, cnt_sc):
    gated = g_ref[0].astype(F32) * (hf_ref[0].astype(F32) + hb_ref[0].astype(F32))
    mix = jnp.dot(gated.astype(BF16), wo_ref[...], preferred_element_type=F32)
    cnt_ref[...] = _residual_and_route(x_ref[0], mix, mods_ref[0, 0], rw_ref, rb_ref, cnt_sc,
                                       xo_ref, hp_ref, meta_ref)


def _rg_out_call(x, mods, gg, hf, hb, w_out, rw, rb):
    b, l, d = x.shape
    t = l // TM
    d_rnn = gg.shape[-1]
    n_exp = rw.shape[1]
    row = lambda bb, tt: (bb, tt, 0)
    const2 = lambda bb, tt: (0, 0)
    return pl.pallas_call(
        _rg_out_kernel,
        grid=(b, t),
        in_specs=[pl.BlockSpec((1, TM, d), row), _mods_spec(d),
                  pl.BlockSpec((1, TM, d_rnn), row),
                  pl.BlockSpec((1, TM, d_rnn), row),
                  pl.BlockSpec((1, TM, d_rnn), row),
                  pl.BlockSpec(w_out.shape, const2),
                  pl.BlockSpec(rw.shape, const2),
                  pl.BlockSpec((1, n_exp), const2)],
        out_specs=_route_out_specs(t, d, n_exp),
        out_shape=_route_out_shapes(b, l, d, n_exp),
        scratch_shapes=[pltpu.VMEM((1, n_exp), F32)],
        compiler_params=_cparams(("arbitrary", "arbitrary")),
        name="rg_out",
    )(x, mods, gg, hf, hb, w_out, rw, rb)


def _dispatch_kernel(pend_ref, padded_ref, dest_ref, hp_ref, xs_ref, zero_sc, sem, zsem, *, n_blocks):
    step = pl.program_id(0)
    n_exp = pend_ref.shape[0]

    @pl.when(step == 0)
    def _():
        zero_sc[...] = jnp.zeros(zero_sc.shape, U32)

        def zero_block(row0):
            rows = pl.ds(pl.multiple_of(row0, MOE_BLK), MOE_BLK)
            return pltpu.make_async_copy(zero_sc, xs_ref.at[rows], zsem)

        def per_expert(e, carry):
            @pl.when(padded_ref[e] > 0)
            def _():
                cp = zero_block(pend_ref[e] - MOE_BLK)
                cp.start()
                cp.wait()
            return carry
        lax.fori_loop(0, n_exp, per_expert, 0)

        def per_tail(i, carry):
            cp = zero_block(i * MOE_BLK)
            cp.start()
            cp.wait()
            return carry
        lax.fori_loop(pend_ref[n_exp - 1] // MOE_BLK, n_blocks, per_tail, 0)

    def body(r, carry):
        for kk in range(TOP_K):
            d = dest_ref[r * TOP_K + kk]
            pltpu.make_async_copy(hp_ref.at[pl.ds(r, 1)], xs_ref.at[pl.ds(d, 1)], sem).start()
        return carry
    lax.fori_loop(0, TM, body, 0)
    for _ in range(TOP_K):
        pltpu.make_async_copy(hp_ref, xs_ref.at[pl.ds(0, TM)], sem).wait()


def _dispatch_call(pend, padded, dest_flat, hp, n_blocks):
    n, c = hp.shape
    kern = functools.partial(_dispatch_kernel, n_blocks=n_blocks)
    return pl.pallas_call(
        kern,
        grid_spec=pltpu.PrefetchScalarGridSpec(
            num_scalar_prefetch=2,
            grid=(n // TM,),
            in_specs=[pl.BlockSpec((TM * TOP_K,), lambda i, *_: (i,), memory_space=pltpu.SMEM),
                      pl.BlockSpec((TM, c), lambda i, *_: (i, 0))],
            out_specs=pl.BlockSpec(memory_space=pl.ANY),
            scratch_shapes=[pltpu.VMEM((MOE_BLK, c), U32),
                            pltpu.SemaphoreType.DMA(()),
                            pltpu.SemaphoreType.DMA(())]),
        out_shape=jax.ShapeDtypeStruct((n_blocks * MOE_BLK, c), U32),
        compiler_params=pltpu.CompilerParams(dimension_semantics=("arbitrary",),
                                             vmem_limit_bytes=VMEM_LIMIT, has_side_effects=True),
        name="moe_dispatch",
    )(pend, padded, dest_flat, hp)


def _expert_kernel(blk_e_ref, xs_ref, wg_ref, wu_ref, bg_ref, bu_ref, wd_ref, bd_ref, ys_ref):
    w = xs_ref[...]
    x = jnp.concatenate([_unpack_hi(w).astype(BF16), _unpack_lo(w).astype(BF16)], axis=-1)
    g = jnp.dot(x, wg_ref[0], preferred_element_type=F32) + bg_ref[0]
    u = jnp.dot(x, wu_ref[0], preferred_element_type=F32) + bu_ref[0]
    g = jnp.minimum(g, SWIGLU_LIMIT)
    u = jnp.clip(u, -SWIGLU_LIMIT, SWIGLU_LIMIT)
    act = (u + 1.0) * (g * _sigmoid(SWIGLU_ALPHA * g))
    y = jnp.dot(act.astype(BF16), wd_ref[0], preferred_element_type=F32) + bd_ref[0]
    ys_ref[...] = _pack_bf16_pair(y)


def _expert_call(blk_e, xs, wg, wu, bg, bu, wd, bd):
    p_rows, c = xs.shape
    n_exp, d, de = wg.shape
    by_e3 = lambda i, be: (be[i], 0, 0)
    return pl.pallas_call(
        _expert_kernel,
        grid_spec=pltpu.PrefetchScalarGridSpec(
            num_scalar_prefetch=1,
            grid=(p_rows // MOE_BLK,),
            in_specs=[pl.BlockSpec((MOE_BLK, c), lambda i, be: (i, 0)),
                      pl.BlockSpec((1, d, de), by_e3),
                      pl.BlockSpec((1, d, de), by_e3),
                      pl.BlockSpec((1, 1, de), by_e3),
                      pl.BlockSpec((1, 1, de), by_e3),
                      pl.BlockSpec((1, de, d), by_e3),
                      pl.BlockSpec((1, 1, d), by_e3)],
            out_specs=pl.BlockSpec((MOE_BLK, c), lambda i, be: (i, 0))),
        out_shape=jax.ShapeDtypeStruct((p_rows, c), U32),
        compiler_params=_cparams(("arbitrary",)),
        name="moe_experts",
    )(blk_e, xs, wg, wu, bg, bu, wd, bd)


def _combine_kernel(dest_ref, meta_ref, x_ref, mods_ref, ys_ref, o_ref, buf, sem):
    def body(r, carry):
        for kk in range(TOP_K):
            d = dest_ref[r * TOP_K + kk]
            pltpu.make_async_copy(ys_ref.at[pl.ds(d, 1)], buf.at[kk, pl.ds(r, 1)], sem).start()
        return carry
    lax.fori_loop(0, TM, body, 0)
    for kk in range(TOP_K):
        pltpu.make_async_copy(ys_ref.at[pl.ds(0, TM)], buf.at[kk], sem).wait()

    meta = meta_ref[...]
    c = buf.shape[-1]
    y_hi = jnp.zeros((TM, c), F32)
    y_lo = jnp.zeros((TM, c), F32)
    for kk in range(TOP_K):
        gk = meta[:, TOP_K + kk:TOP_K + kk + 1]
        w = buf[kk]
        y_hi = y_hi + gk * _unpack_hi(w)
        y_lo = y_lo + gk * _unpack_lo(w)
    gf = mods_ref[0, 0][5:6, :]
    o_ref[0, :, :c] = x_ref[0, :, :c] + gf[:, :c] * y_hi
    o_ref[0, :, c:] = x_ref[0, :, c:] + gf[:, c:] * y_lo


def _combine_call(dest_flat, meta, x, mods, ys):
    b, l, d = x.shape
    t = l // TM
    c = ys.shape[-1]
    row = lambda bb, tt: (bb, tt, 0)
    return pl.pallas_call(
        _combine_kernel,
        grid=(b, t),
        in_specs=[pl.BlockSpec((TM * TOP_K,), lambda bb, tt: (bb * t + tt,), memory_space=pltpu.SMEM),
                  pl.BlockSpec((TM, LANES), lambda bb, tt: (bb * t + tt, 0)),
                  pl.BlockSpec((1, TM, d), row),
                  _mods_spec(d),
                  pl.BlockSpec(memory_space=pl.ANY)],
        out_specs=pl.BlockSpec((1, TM, d), row),
        out_shape=jax.ShapeDtypeStruct((b, l, d), F32),
        scratch_shapes=[pltpu.VMEM((TOP_K, TM, c), U32), pltpu.SemaphoreType.DMA(())],
        compiler_params=_cparams(("arbitrary", "arbitrary")),
        name="moe_combine",
    )(dest_flat, meta, x, mods, ys)


def _moe(x_new, hp, meta, counts, mods, wg, wu, bg, bu, wd, bd):
    n = hp.shape[0]
    n_exp = wg.shape[0]
    n_blocks = -(-n * TOP_K // MOE_BLK) + n_exp
    counts = counts.reshape(n_exp).astype(jnp.int32)
    padded = (counts + MOE_BLK - 1) // MOE_BLK * MOE_BLK
    pend = jnp.cumsum(padded).astype(jnp.int32)
    pstart = pend - padded
    top_e = meta[:, 0:TOP_K].astype(jnp.int32)
    rank = meta[:, 2 * TOP_K:3 * TOP_K].astype(jnp.int32)
    dest = (pstart[top_e] + rank).reshape(n * TOP_K)
    blk_e = jnp.minimum(jnp.searchsorted(pend, jnp.arange(n_blocks, dtype=jnp.int32) * MOE_BLK,
                                         side='right'), n_exp - 1).astype(jnp.int32)
    xs = _dispatch_call(pend, padded, dest, hp, n_blocks)
    ys = _expert_call(blk_e, xs, wg, wu, bg, bu, wd, bd)
    return _combine_call(dest, meta, x_new, mods, ys)


def _final_norm_kernel(x_ref, g_ref, o_ref):
    x = x_ref[0]
    o_ref[0] = x * lax.rsqrt(jnp.mean(x * x, axis=-1, keepdims=True) + EPS) * g_ref[...]


def _final_norm_call(x, gain, n_ctx):
    b, l, d = x.shape
    s = l - n_ctx
    skip = n_ctx // TM
    return pl.pallas_call(
        _final_norm_kernel,
        grid=(b, s // TM),
        in_specs=[pl.BlockSpec((1, TM, d), lambda bb, tt: (bb, tt + skip, 0)),
                  pl.BlockSpec((1, d), lambda bb, tt: (0, 0))],
        out_specs=pl.BlockSpec((1, TM, d), lambda bb, tt: (bb, tt, 0)),
        out_shape=jax.ShapeDtypeStruct((b, s, d), F32),
        compiler_params=_cparams(("parallel", "parallel")),
        name="final_norm",
    )(x, gain.reshape(1, d))


def _rope_tables(s, n_ctx):
    t = jnp.arange(s, dtype=jnp.int32)
    inv_freq = ROPE_THETA ** (-jnp.arange(ROPE_AXIS_DIM // 2, dtype=F32) * 2.0 / ROPE_AXIS_DIM)
    ang_r = (t // GRID_W).astype(F32)[:, None] * inv_freq
    ang_c = (t % GRID_W).astype(F32)[:, None] * inv_freq
    cr, sr, cc, sc = jnp.cos(ang_r), jnp.sin(ang_r), jnp.cos(ang_c), jnp.sin(ang_c)
    cos_t = jnp.concatenate([cr, cr, cc, cc], axis=-1)
    sin_t = jnp.concatenate([-sr, sr, -sc, sc], axis=-1)
    cos_t = jnp.concatenate([jnp.ones((n_ctx, HEAD_DIM), F32), cos_t], axis=0)
    sin_t = jnp.concatenate([jnp.zeros((n_ctx, HEAD_DIM), F32), sin_t], axis=0)
    return cos_t, sin_t


def kernel(x, c, ctx, c_ctx, mod_w, mod_b, ap_w_in, ap_q_gain, ap_k_gain, ap_pool_w, ap_pool_scale, ap_w_out, rg_w_in, rg_conv_w, rg_conv_b, rg_gate_r_w, rg_gate_r_b, rg_gate_i_w, rg_gate_i_b, rg_lambda, rg_w_out, moe_router_w, moe_router_b, moe_w_gu, moe_b_gu, moe_w_down, moe_b_down, final_norm_g):
    b, s, d = x.shape
    n_ctx = ctx.shape[1]
    depth = mod_w.shape[0]
    assert n_ctx % TM == 0 and s % TM == 0 and s % ATT_CK == 0 and s % GRID_W == 0
    assert b + 1 <= SUBLANES and d % (2 * LANES) == 0

    cv = jnp.zeros((SUBLANES, d), F32).at[:b].set(c).at[b].set(c_ctx)
    mods_all = _mods_call(cv, mod_w, mod_b).reshape(depth, SUBLANES, 6, d)
    cos_t, sin_t = _rope_tables(s, n_ctx)

    xx = jnp.concatenate([ctx, x], axis=1)
    d_rnn = rg_w_out.shape[1]
    rnn_blk = rg_gate_r_w.shape[-1]
    assert d_rnn % SCAN_LANES == 0 and d_rnn % (3 * LANES) == 0

    for l in range(depth):
        i = l // 2
        ml = mods_all[l]
        mods = jnp.stack([jnp.broadcast_to(ml[b][None], (b, 6, d)), ml[:b]])
        rw = moe_router_w[l]
        rb = moe_router_b[l].reshape(1, -1)
        if l % 2 == 0:
            q, k, v, p = _attn_in_call(xx, mods, ap_w_in[i].astype(BF16),
                                       ap_q_gain[i].reshape(1, -1), ap_k_gain[i].reshape(1, -1),
                                       cos_t, sin_t)
            ao = _attention_call(q, k, v, n_ctx)
            wo = ap_w_out[i].astype(BF16)
            x_new, hp, meta, counts = _attn_out_call(
                xx, mods, ao, p, ap_pool_w[i].astype(BF16), ap_pool_scale[i].reshape(1, -1),
                wo[:ATTN_DIM], wo[ATTN_DIM:], rw, rb, n_ctx)
        else:
            w_in = rg_w_in[i].astype(BF16)
            gg = _rg_in_call(xx, mods, w_in[:, :d_rnn], True, BF16, "rg_in_gate")
            u = _rg_in_call(xx, mods, w_in[:, d_rnn:], False, F32, "rg_in_u")
            hs = []
            for dr in range(2):
                slabs = _gate_slabs(rg_gate_r_w[i, dr], rg_gate_i_w[i, dr], d_rnn)
                gate_b = jnp.stack([rg_gate_r_b[i, dr], rg_gate_i_b[i, dr]])
                hs.append(_rg_scan_call(u, rg_conv_w[i], rg_conv_b[i].reshape(1, -1), slabs, gate_b,
                                        rg_lambda[i, dr], dr == 1, rnn_blk))
            x_new, hp, meta, counts = _rg_out_call(xx, mods, gg, hs[0], hs[1],
                                                   rg_w_out[i].astype(BF16), rw, rb)
        de = moe_w_down.shape[2]
        xx = _moe(x_new, hp, meta, counts, mods,
                  moe_w_gu[l][:, :, 0::2].astype(BF16), moe_w_gu[l][:, :, 1::2].astype(BF16),
                  moe_b_gu[l][:, 0::2].reshape(-1, 1, de), moe_b_gu[l][:, 1::2].reshape(-1, 1, de),
                  moe_w_down[l].astype(BF16), moe_b_down[l].reshape(-1, 1, d))

    return _final_norm_call(xx, final_norm_g, n_ctx)
```

```python
import functools

import jax
import jax.numpy as jnp
import numpy as np
from jax import lax
from jax.experimental import pallas as pl
from jax.experimental.pallas import tpu as pltpu

F32 = jnp.float32
BF16 = jnp.bfloat16
U32 = jnp.uint32

EPS = 1e-6
GRID_W = 64
HEAD_DIM = 128
N_Q_HEADS = 12
N_KV_HEADS = 4
Q_PER_KV = N_Q_HEADS // N_KV_HEADS
ATTN_DIM = N_Q_HEADS * HEAD_DIM
KV_DIM = N_KV_HEADS * HEAD_DIM
ROPE_THETA = 10000.0
ROPE_AXIS_DIM = HEAD_DIM // 2
POOL_WINDOWS = (2, 4, 8, 16)
N_POOL_GROUPS = 4
N_RNN_BLOCKS = 16
CONV_WIDTH = 4
CONV_LEFT = 2
RG_C = 8.0
TOP_K = 4
SWIGLU_LIMIT = 7.0
SWIGLU_ALPHA = 1.702

LANES = 128
SUBLANES = 8
MXU_DIM = 256
VMEM_LIMIT = 56 * 1024 * 1024

TM = 256
HALO = SUBLANES
ATT_CK = 1024
LOG2E = 1.4426950408889634
MOE_BLK = 512
GATE_TN = 256
SCAN_LANES = 896


def _cparams(sem):
    return pltpu.CompilerParams(dimension_semantics=sem, vmem_limit_bytes=VMEM_LIMIT)


def _sigmoid(x):
    return 1.0 / (1.0 + jnp.exp(-x))


def _norm_mod(x, mods, i_shift, i_scale):
    ms = jnp.mean(x * x, axis=-1, keepdims=True)
    xn = x * lax.rsqrt(ms + EPS)
    return xn * (1.0 + mods[i_scale:i_scale + 1, :]) + mods[i_shift:i_shift + 1, :]


def _pack_bf16_pair(h):
    c = h.shape[-1] // 2
    bits = pltpu.bitcast(h.astype(BF16).astype(F32), U32)
    return (bits[:, :c] & jnp.uint32(0xFFFF0000)) | (bits[:, c:] >> jnp.uint32(16))


def _unpack_hi(w):
    return pltpu.bitcast(w & jnp.uint32(0xFFFF0000), F32)


def _unpack_lo(w):
    return pltpu.bitcast(w << jnp.uint32(16), F32)


def _mods_kernel(cv_ref, w_ref, b_ref, o_ref):
    cv = cv_ref[...]
    s = cv * _sigmoid(cv)
    o_ref[0] = jnp.dot(s.astype(BF16), w_ref[0].astype(BF16), preferred_element_type=F32) + b_ref[0]


def _mods_call(cv, mod_w, mod_b):
    depth, d, n6 = mod_w.shape
    tn = 1536
    return pl.pallas_call(
        _mods_kernel,
        grid=(depth, n6 // tn),
        in_specs=[pl.BlockSpec((SUBLANES, d), lambda l, j: (0, 0)),
                  pl.BlockSpec((1, d, tn), lambda l, j: (l, 0, j)),
                  pl.BlockSpec((1, 1, tn), lambda l, j: (l, 0, j))],
        out_specs=pl.BlockSpec((1, SUBLANES, tn), lambda l, j: (l, 0, j)),
        out_shape=jax.ShapeDtypeStruct((depth, SUBLANES, n6), F32),
        compiler_params=_cparams(("parallel", "parallel")),
        name="adaln_mods",
    )(cv, mod_w, mod_b.reshape(depth, 1, n6))


def _mods_spec(d):
    return pl.BlockSpec((1, 1, 6, d), lambda b, t: (jnp.minimum(t, 1), b, 0, 0))


def _attn_in_kernel(x_ref, mods_ref, w_ref, qg_ref, kg_ref, cos_ref, sin_ref,
                    q_ref, k_ref, v_ref, p_ref):
    h = _norm_mod(x_ref[0], mods_ref[0, 0], 0, 1).astype(BF16)
    z = jnp.dot(h, w_ref[...], preferred_element_type=F32)
    cos = cos_ref[...]
    sin = sin_ref[...]
    lane = lax.broadcasted_iota(jnp.int32, (TM, HEAD_DIM), 1)
    first = (lane % (ROPE_AXIS_DIM)) < (ROPE_AXIS_DIM // 2)

    def head(zh, gain, scale):
        n = zh * lax.rsqrt(jnp.mean(zh * zh, axis=-1, keepdims=True) + EPS) * gain
        half = ROPE_AXIS_DIM // 2
        swapped = jnp.where(first, pltpu.roll(n, HEAD_DIM - half, 1), pltpu.roll(n, half, 1))
        return (n * cos + swapped * sin) * scale

    qg = qg_ref[...]
    kg = kg_ref[...]
    for hh in range(N_Q_HEADS):
        sl = slice(hh * HEAD_DIM, (hh + 1) * HEAD_DIM)
        q_ref[0, :, sl] = head(z[:, sl], qg, LOG2E * HEAD_DIM ** -0.5).astype(BF16)
    for hh in range(N_KV_HEADS):
        zs = slice(ATTN_DIM + hh * HEAD_DIM, ATTN_DIM + (hh + 1) * HEAD_DIM)
        k_ref[0, hh] = head(z[:, zs], kg, 1.0).T.astype(BF16)
    v_ref[0] = z[:, ATTN_DIM + KV_DIM:ATTN_DIM + 2 * KV_DIM].astype(BF16)
    p_ref[0] = z[:, ATTN_DIM + 2 * KV_DIM:]


def _attn_in_call(x, mods, w_in, q_gain, k_gain, cos_t, sin_t):
    b, l, d = x.shape
    t = l // TM
    n_in = w_in.shape[1]
    pool_dim = n_in - ATTN_DIM - 2 * KV_DIM
    row = lambda bb, tt: (bb, tt, 0)
    return pl.pallas_call(
        _attn_in_kernel,
        grid=(b, t),
        in_specs=[pl.BlockSpec((1, TM, d), row),
                  _mods_spec(d),
                  pl.BlockSpec((d, n_in), lambda bb, tt: (0, 0)),
                  pl.BlockSpec((1, HEAD_DIM), lambda bb, tt: (0, 0)),
                  pl.BlockSpec((1, HEAD_DIM), lambda bb, tt: (0, 0)),
                  pl.BlockSpec((TM, HEAD_DIM), lambda bb, tt: (tt, 0)),
                  pl.BlockSpec((TM, HEAD_DIM), lambda bb, tt: (tt, 0))],
        out_specs=[pl.BlockSpec((1, TM, ATTN_DIM), row),
                   pl.BlockSpec((1, N_KV_HEADS, HEAD_DIM, TM), lambda bb, tt: (bb, 0, 0, tt)),
                   pl.BlockSpec((1, TM, KV_DIM), row),
                   pl.BlockSpec((1, TM, pool_dim), row)],
        out_shape=[jax.ShapeDtypeStruct((b, l, ATTN_DIM), BF16),
                   jax.ShapeDtypeStruct((b, N_KV_HEADS, HEAD_DIM, l), BF16),
                   jax.ShapeDtypeStruct((b, l, KV_DIM), BF16),
                   jax.ShapeDtypeStruct((b, l, pool_dim), F32)],
        compiler_params=_cparams(("parallel", "parallel")),
        name="attn_in",
    )(x, mods, w_in, q_gain, k_gain, cos_t, sin_t)


def _attention_kernel(q_ref, kt_ref, v_ref, o_ref, sa_sc, sb_sc, m_sc, acc_sc, *, n_ctx, n_lat_chunks):
    t = pl.program_id(2)
    q = jnp.concatenate([q_ref[0, :, g * HEAD_DIM:(g + 1) * HEAD_DIM] for g in range(Q_PER_KV)],
                        axis=0)
    m_sc[...] = jnp.full(m_sc.shape, -1e30, F32)
    acc_sc[...] = jnp.zeros(acc_sc.shape, F32)

    def scores(start, size):
        return jnp.dot(q, kt_ref[0, 0, :, pl.ds(start, size)], preferred_element_type=F32)

    def v_ext(start, size):
        return jnp.concatenate([v_ref[0, pl.ds(start, size), :], jnp.ones((size, HEAD_DIM), BF16)],
                               axis=-1)

    def softmax_pv(load_s, vx, width):
        for g in range(Q_PER_KV):
            rows = slice(g * TM, (g + 1) * TM)
            s = load_s(rows)
            blocks = [s[:, c * LANES:(c + 1) * LANES] for c in range(width // LANES)]
            mc = blocks[0]
            for blk in blocks[1:]:
                mc = jnp.maximum(mc, blk)
            m_prev = m_sc[rows, :]
            m_new = jnp.maximum(m_prev, jnp.broadcast_to(jnp.max(mc, axis=-1, keepdims=True),
                                                         (TM, LANES)))
            alpha = jnp.exp2(m_prev - m_new)
            p = jnp.concatenate([jnp.exp2(blk - m_new).astype(BF16) for blk in blocks], axis=-1)
            pv = jnp.dot(p, vx, preferred_element_type=F32)
            acc_sc[rows, :HEAD_DIM] = alpha * acc_sc[rows, :HEAD_DIM] + pv[:, :HEAD_DIM]
            acc_sc[rows, HEAD_DIM:] = alpha * acc_sc[rows, HEAD_DIM:] + pv[:, HEAD_DIM:]
            m_sc[rows, :] = m_new

    s0 = scores(0, n_ctx)
    softmax_pv(lambda rows: s0[rows], v_ext(0, n_ctx), n_ctx)

    @pl.when(t > 0)
    def _():
        def start(c):
            return pl.multiple_of(n_ctx + c * ATT_CK, LANES)

        def pair(c, prefetch):
            sb_sc[...] = scores(start(c + 1), ATT_CK)
            softmax_pv(lambda rows: sa_sc[rows, :], v_ext(start(c), ATT_CK), ATT_CK)
            if prefetch:
                sa_sc[...] = scores(start(c + 2), ATT_CK)
            softmax_pv(lambda rows: sb_sc[rows, :], v_ext(start(c + 1), ATT_CK), ATT_CK)

        sa_sc[...] = scores(n_ctx, ATT_CK)

        def body(i, carry):
            pair(2 * i, True)
            return carry
        lax.fori_loop(0, n_lat_chunks // 2 - 1, body, 0)
        pair(n_lat_chunks - 2, False)

    for g in range(Q_PER_KV):
        rows = slice(g * TM, (g + 1) * TM)
        o = acc_sc[rows, :HEAD_DIM] / acc_sc[rows, HEAD_DIM:]
        o_ref[0, :, g * HEAD_DIM:(g + 1) * HEAD_DIM] = o.astype(BF16)


def _attention_call(q, k, v, n_ctx):
    b, l, _ = q.shape
    t = l // TM
    gw = Q_PER_KV * HEAD_DIM
    kern = functools.partial(_attention_kernel, n_ctx=n_ctx, n_lat_chunks=(l - n_ctx) // ATT_CK)
    return pl.pallas_call(
        kern,
        grid=(b, N_KV_HEADS, t),
        in_specs=[pl.BlockSpec((1, TM, gw), lambda bb, hh, tt: (bb, tt, hh)),
                  pl.BlockSpec((1, 1, HEAD_DIM, l), lambda bb, hh, tt: (bb, hh, 0, 0)),
                  pl.BlockSpec((1, l, HEAD_DIM), lambda bb, hh, tt: (bb, 0, hh))],
        out_specs=pl.BlockSpec((1, TM, gw), lambda bb, hh, tt: (bb, tt, hh)),
        out_shape=jax.ShapeDtypeStruct((b, l, ATTN_DIM), BF16),
        scratch_shapes=[pltpu.VMEM((Q_PER_KV * TM, ATT_CK), F32),
                        pltpu.VMEM((Q_PER_KV * TM, ATT_CK), F32),
                        pltpu.VMEM((Q_PER_KV * TM, LANES), F32),
                        pltpu.VMEM((Q_PER_KV * TM, 2 * HEAD_DIM), F32)],
        compiler_params=_cparams(("parallel", "parallel", "parallel")),
        name="attention",
    )(q, k, v)


def _residual_and_route(x, mix, mods, rw_ref, rb_ref, cnt_sc, xo_ref, hp_ref, meta_ref):
    first_step = (pl.program_id(0) == 0) & (pl.program_id(1) == 0)

    @pl.when(first_step)
    def _():
        cnt_sc[...] = jnp.zeros(cnt_sc.shape, F32)

    xnew = x + mods[2:3, :] * mix
    xo_ref[0] = xnew
    h2 = _norm_mod(xnew, mods, 3, 4)
    hp_ref[...] = _pack_bf16_pair(h2)

    n_exp = rw_ref.shape[-1]
    h_hi = h2.astype(BF16)
    h_lo = (h2 - h_hi.astype(F32)).astype(BF16)
    logits = (jnp.dot(h_hi, rw_ref[0], preferred_element_type=F32)
              + jnp.dot(h_lo, rw_ref[0], preferred_element_type=F32)
              + jnp.dot(h_hi, rw_ref[1], preferred_element_type=F32)) + rb_ref[...]
    lane = lax.broadcasted_iota(jnp.int32, (TM, n_exp), 1).astype(F32)
    onehots, tops, idxs = [], [], []
    cur = logits
    for _ in range(TOP_K):
        m = jnp.max(cur, axis=-1, keepdims=True)
        idx = jnp.min(jnp.where(cur == m, lane, float(n_exp)), axis=-1, keepdims=True)
        sel = lane == idx
        tops.append(m)
        idxs.append(idx)
        onehots.append(sel)
        cur = jnp.where(sel, -jnp.inf, cur)
    ws = [jnp.exp(m - tops[0]) for m in tops]
    denom = ws[0] + ws[1] + ws[2] + ws[3]

    counts = jnp.zeros((TM, n_exp), F32)
    for sel in onehots:
        counts = counts + jnp.where(sel, 1.0, 0.0)
    ri = lax.broadcasted_iota(jnp.int32, (TM, TM), 0)
    ci = lax.broadcasted_iota(jnp.int32, (TM, TM), 1)
    tri = jnp.where(ci < ri, 1.0, 0.0).astype(BF16)
    before = jnp.dot(tri, counts.astype(BF16), preferred_element_type=F32) + cnt_sc[...]

    lane_m = lax.broadcasted_iota(jnp.int32, (TM, LANES), 1)
    meta = jnp.zeros((TM, LANES), F32)
    for kk in range(TOP_K):
        rank = jnp.sum(jnp.where(onehots[kk], before, 0.0), axis=-1, keepdims=True)
        meta = jnp.where(lane_m == kk, idxs[kk], meta)
        meta = jnp.where(lane_m == TOP_K + kk, ws[kk] / denom, meta)
        meta = jnp.where(lane_m == 2 * TOP_K + kk, rank, meta)
    meta_ref[...] = meta
    cnt_sc[...] = cnt_sc[...] + jnp.sum(counts, axis=0, keepdims=True)
    return cnt_sc[...]


def _route_out_specs(t, d, n_exp):
    flat = lambda bb, tt: (bb * t + tt, 0)
    return [pl.BlockSpec((1, TM, d), lambda bb, tt: (bb, tt, 0)),
            pl.BlockSpec((TM, d // 2), flat),
            pl.BlockSpec((TM, LANES), flat),
            pl.BlockSpec((1, n_exp), lambda bb, tt: (0, 0))]


def _route_out_shapes(b, l, d, n_exp):
    return [jax.ShapeDtypeStruct((b, l, d), F32),
            jax.ShapeDtypeStruct((b * l, d // 2), U32),
            jax.ShapeDtypeStruct((b * l, LANES), F32),
            jax.ShapeDtypeStruct((1, n_exp), F32)]


def _attn_out_kernel(x_ref, mods_ref, ao_ref, p_ref, pp_ref, pn_ref, pw_ref, ps_ref,
                     woa_ref, wop_ref, rw_ref, rb_ref,
                     xo_ref, hp_ref, meta_ref, cnt_ref, e_sc, cnt_sc, *, n_ctx, n_lat):
    t = pl.program_id(1)
    nt = pl.num_programs(1)
    prev_ok = t >= 2
    next_ok = (t >= 1) & (t <= nt - 2)
    e_sc[0:HALO, :] = jnp.where(prev_ok, pp_ref[0], 0.0)
    e_sc[HALO:HALO + TM, :] = p_ref[0]
    e_sc[HALO + TM:, :] = jnp.where(next_ok, pn_ref[0], 0.0)

    rowi = lax.broadcasted_iota(jnp.int32, (TM, 1), 0)
    pos = jnp.where(t == 0, rowi, (t - 1) * TM + rowi)
    seq_len = jnp.where(t == 0, n_ctx, n_lat)
    gw = p_ref.shape[-1] // N_POOL_GROUPS
    mix = jnp.dot(ao_ref[0], woa_ref[...], preferred_element_type=F32)
    for g, w in enumerate(POOL_WINDOWS):
        ls = slice(g * gw, (g + 1) * gw)
        acc = e_sc[HALO - w // 2:HALO - w // 2 + TM, ls]
        for j in range(1, w):
            off = HALO - w // 2 + j
            acc = acc + e_sc[off:off + TM, ls]
        lo = jnp.maximum(pos - w // 2, 0)
        hi = jnp.minimum(pos + w - w // 2, seq_len)
        dlt = acc / (hi - lo).astype(F32) - e_sc[HALO:HALO + TM, ls]
        y = jnp.dot(dlt.astype(BF16), pw_ref[g], preferred_element_type=F32) * ps_ref[:, ls]
        mix = mix + jnp.dot(y.astype(BF16), wop_ref[ls, :], preferred_element_type=F32)

    cnt_ref[...] = _residual_and_route(x_ref[0], mix, mods_ref[0, 0], rw_ref, rb_ref, cnt_sc,
                                       xo_ref, hp_ref, meta_ref)


def _attn_out_call(x, mods, ao, p, pool_w, pool_scale, wo_a, wo_p, rw, rb, n_ctx):
    b, l, d = x.shape
    t = l // TM
    pool_dim = p.shape[-1]
    n_exp = rw.shape[-1]
    nh = l // HALO
    per = TM // HALO
    row = lambda bb, tt: (bb, tt, 0)
    const2 = lambda bb, tt: (0, 0)
    kern = functools.partial(_attn_out_kernel, n_ctx=n_ctx, n_lat=l - n_ctx)
    return pl.pallas_call(
        kern,
        grid=(b, t),
        in_specs=[pl.BlockSpec((1, TM, d), row),
                  _mods_spec(d),
                  pl.BlockSpec((1, TM, ATTN_DIM), row),
                  pl.BlockSpec((1, TM, pool_dim), row),
                  pl.BlockSpec((1, HALO, pool_dim), lambda bb, tt: (bb, jnp.maximum(tt * per - 1, 0), 0)),
                  pl.BlockSpec((1, HALO, pool_dim), lambda bb, tt: (bb, jnp.minimum((tt + 1) * per, nh - 1), 0)),
                  pl.BlockSpec(pool_w.shape, lambda bb, tt: (0, 0, 0)),
                  pl.BlockSpec((1, pool_dim), const2),
                  pl.BlockSpec(wo_a.shape, const2),
                  pl.BlockSpec(wo_p.shape, const2),
                  pl.BlockSpec(rw.shape, lambda bb, tt: (0, 0, 0)),
                  pl.BlockSpec((1, n_exp), const2)],
        out_specs=_route_out_specs(t, d, n_exp),
        out_shape=_route_out_shapes(b, l, d, n_exp),
        scratch_shapes=[pltpu.VMEM((TM + 2 * HALO, pool_dim), F32),
                        pltpu.VMEM((1, n_exp), F32)],
        compiler_params=_cparams(("arbitrary", "arbitrary")),
        name="attn_out",
    )(x, mods, ao, p, p, p, pool_w, pool_scale, wo_a, wo_p, rw, rb)


def _gelu_tanh(x):
    return 0.5 * x * (1.0 + jnp.tanh(0.7978845608028654 * (x + 0.044715 * (x * x * x))))


def _rg_in_kernel(x_ref, mods_ref, w_ref, o_ref, *, gelu):
    h = _norm_mod(x_ref[0], mods_ref[0, 0], 0, 1).astype(BF16)
    z = jnp.dot(h, w_ref[...], preferred_element_type=F32)
    if gelu:
        z = _gelu_tanh(z)
    o_ref[0] = z.astype(o_ref.dtype)


def _rg_in_call(x, mods, w, gelu, out_dtype, name):
    b, l, d = x.shape
    t = l // TM
    n = w.shape[1]
    row = lambda bb, tt: (bb, tt, 0)
    return pl.pallas_call(
        functools.partial(_rg_in_kernel, gelu=gelu),
        grid=(b, t),
        in_specs=[pl.BlockSpec((1, TM, d), row), _mods_spec(d),
                  pl.BlockSpec((d, n), lambda bb, tt: (0, 0))],
        out_specs=pl.BlockSpec((1, TM, n), row),
        out_shape=jax.ShapeDtypeStruct((b, l, n), out_dtype),
        compiler_params=_cparams(("parallel", "parallel")),
        name=name,
    )(x, mods, w)


def _gate_slab_starts(d_rnn, blk):
    n_slabs = -(-d_rnn // GATE_TN)
    spans = []
    for j in range(n_slabs):
        c0, c1 = j * GATE_TN, min((j + 1) * GATE_TN, d_rnn)
        lo = (c0 // blk) * blk
        hi = ((c1 - 1) // blk + 1) * blk
        spans.append((lo // LANES * LANES, -(-hi // LANES) * LANES))
    km = max(h - l for l, h in spans)
    starts = [min(l, d_rnn - km) for l, _ in spans]
    return starts, km


def _gate_slabs(r_w, i_w, d_rnn):
    blk = r_w.shape[-1]
    starts, km = _gate_slab_starts(d_rnn, blk)
    n_slabs = len(starts)
    pad = n_slabs * GATE_TN - d_rnn
    eye = jnp.eye(r_w.shape[0], dtype=r_w.dtype)
    dense = [jnp.pad((w[:, :, None, :] * eye[:, None, :, None]).reshape(d_rnn, d_rnn),
                     ((0, 0), (0, pad))) for w in (r_w, i_w)]
    slabs = [jnp.concatenate([dd[s:s + km, j * GATE_TN:(j + 1) * GATE_TN] for dd in dense], axis=1)
             for j, s in enumerate(starts)]
    return jnp.stack(slabs).astype(BF16)


def _rg_scan_kernel(u_ref, up_ref, un_ref, cw_ref, cb_ref, w_ref, gb_ref, lam_ref, h_ref,
                    e_sc, ucb_sc, a_sc, b_sc, carry_sc, *, reverse, starts, km):
    s = pl.program_id(1)
    nt = pl.num_programs(1)
    tau = jnp.where(s == 0, 0, nt - s) if reverse else s
    d_rnn = u_ref.shape[-1]

    @pl.when(s == 0)
    def _():
        carry_sc[...] = jnp.zeros(carry_sc.shape, F32)

    prev_ok = tau >= 2
    next_ok = (tau >= 1) & (tau <= nt - 2)
    e_sc[0:HALO, :] = jnp.where(prev_ok, up_ref[0], 0.0)
    e_sc[HALO:HALO + TM, :] = u_ref[0]
    e_sc[HALO + TM:, :] = jnp.where(next_ok, un_ref[0], 0.0)
    cw = cw_ref[...]
    n_conv_chunks = d_rnn // (3 * LANES)
    for c in range(n_conv_chunks):
        ls = slice(c * 3 * LANES, (c + 1) * 3 * LANES)
        uc = cb_ref[:, ls] + cw[0:1, ls] * e_sc[HALO - CONV_LEFT:HALO - CONV_LEFT + TM, ls]
        for j in range(1, CONV_WIDTH):
            off = HALO - CONV_LEFT + j
            uc = uc + cw[j:j + 1, ls] * e_sc[off:off + TM, ls]
        a_sc[:, ls] = uc
        ucb_sc[:, ls] = uc.astype(BF16)

    lam = lam_ref[0]
    neg = -lam
    softplus = jnp.maximum(neg, 0.0) + jnp.log(1.0 + jnp.exp(-jnp.abs(neg)))
    gb = gb_ref[0]
    for j, st in enumerate(starts):
        n = min(GATE_TN, d_rnn - j * GATE_TN)
        ls = slice(j * GATE_TN, j * GATE_TN + n)
        z = jnp.dot(ucb_sc[:, st:st + km], w_ref[0, j], preferred_element_type=F32)
        r = _sigmoid(z[:, :n] + gb[0:1, ls])
        i = _sigmoid(z[:, GATE_TN:GATE_TN + n] + gb[1:2, ls])
        log_a = (-RG_C) * r * softplus[:, ls]
        a = jnp.exp(log_a)
        mult = jnp.sqrt(jnp.maximum(1.0 - a * a, 0.0))
        b_sc[:, ls] = mult * (i * a_sc[:, ls])
        a_sc[:, ls] = a

    n_groups = TM // SUBLANES
    rowi = lax.broadcasted_iota(jnp.int32, (SUBLANES, SCAN_LANES), 0)
    for c in range(d_rnn // SCAN_LANES):
        ls = slice(c * SCAN_LANES, (c + 1) * SCAN_LANES)

        def body(gi, h):
            g = (n_groups - 1 - gi) if reverse else gi
            rows = pl.ds(pl.multiple_of(g * SUBLANES, SUBLANES), SUBLANES)
            a = a_sc[rows, ls]
            b = b_sc[rows, ls]
            for sft in (1, 2, 4):
                if reverse:
                    ok = rowi < SUBLANES - sft
                    shift = SUBLANES - sft
                else:
                    ok = rowi >= sft
                    shift = sft
                ap = jnp.where(ok, pltpu.roll(a, shift, 0), 1.0)
                bp = jnp.where(ok, pltpu.roll(b, shift, 0), 0.0)
                b = a * bp + b
                a = a * ap
            hh = a * h + b
            b_sc[rows, ls] = hh
            last = hh[0:1, :] if reverse else hh[SUBLANES - 1:SUBLANES, :]
            return jnp.broadcast_to(last, (SUBLANES, SCAN_LANES))

        carry_sc[:, ls] = lax.fori_loop(0, n_groups, body, carry_sc[:, ls])

    h_ref[0] = b_sc[...].astype(h_ref.dtype)


def _rg_scan_call(u, conv_w, conv_b, slabs, gate_b, lam, reverse, blk):
    b, l, d_rnn = u.shape
    t = l // TM
    nh = l // HALO
    per = TM // HALO
    starts, km = _gate_slab_starts(d_rnn, blk)

    def tile(ss):
        return jnp.where(ss == 0, 0, t - ss) if reverse else ss

    kern = functools.partial(_rg_scan_kernel, reverse=reverse, starts=tuple(starts), km=km)
    const2 = lambda bb, ss: (0, 0)
    return pl.pallas_call(
        kern,
        grid=(b, t),
        in_specs=[pl.BlockSpec((1, TM, d_rnn), lambda bb, ss: (bb, tile(ss), 0)),
                  pl.BlockSpec((1, HALO, d_rnn),
                               lambda bb, ss: (bb, jnp.maximum(tile(ss) * per - 1, 0), 0)),
                  pl.BlockSpec((1, HALO, d_rnn),
                               lambda bb, ss: (bb, jnp.minimum((tile(ss) + 1) * per, nh - 1), 0)),
                  pl.BlockSpec(conv_w.shape, const2),
                  pl.BlockSpec((1, d_rnn), const2),
                  pl.BlockSpec((1,) + slabs.shape, lambda bb, ss: (0, 0, 0, 0)),
                  pl.BlockSpec((1, 2, d_rnn), lambda bb, ss: (0, 0, 0)),
                  pl.BlockSpec((1, 1, d_rnn), lambda bb, ss: (0, 0, 0))],
        out_specs=pl.BlockSpec((1, TM, d_rnn), lambda bb, ss: (bb, tile(ss), 0)),
        out_shape=jax.ShapeDtypeStruct((b, l, d_rnn), BF16),
        scratch_shapes=[pltpu.VMEM((TM + 2 * HALO, d_rnn), F32),
                        pltpu.VMEM((TM, d_rnn), BF16),
                        pltpu.VMEM((TM, d_rnn), F32),
                        pltpu.VMEM((TM, d_rnn), F32),
                        pltpu.VMEM((SUBLANES, d_rnn), F32)],
        compiler_params=_cparams(("arbitrary", "arbitrary")),
        name="rg_scan_bwd" if reverse else "rg_scan_fwd",
    )(u, u, u, conv_w, conv_b, slabs[None], gate_b[None], lam[None, None])


def _rg_out_kernel(x_ref, mods_ref, g_ref, hf_ref, hb_ref, wo_ref, rw_ref, rb_ref,
                   xo_ref, hp_ref, meta_ref, cnt_ref, cnt_sc):
    gated = g_ref[0].astype(F32) * (hf_ref[0].astype(F32) + hb_ref[0].astype(F32))
    mix = jnp.dot(gated.astype(BF16), wo_ref[...], preferred_element_type=F32)
    cnt_ref[...] = _residual_and_route(x_ref[0], mix, mods_ref[0, 0], rw_ref, rb_ref, cnt_sc,
                                       xo_ref, hp_ref, meta_ref)


def _rg_out_call(x, mods, gg, hf, hb, w_out, rw, rb):
    b, l, d = x.shape
    t = l // TM
    d_rnn = gg.shape[-1]
    n_exp = rw.shape[-1]
    row = lambda bb, tt: (bb, tt, 0)
    const2 = lambda bb, tt: (0, 0)
    return pl.pallas_call(
        _rg_out_kernel,
        grid=(b, t),
        in_specs=[pl.BlockSpec((1, TM, d), row), _mods_spec(d),
                  pl.BlockSpec((1, TM, d_rnn), row),
                  pl.BlockSpec((1, TM, d_rnn), row),
                  pl.BlockSpec((1, TM, d_rnn), row),
                  pl.BlockSpec(w_out.shape, const2),
                  pl.BlockSpec(rw.shape, lambda bb, tt: (0, 0, 0)),
                  pl.BlockSpec((1, n_exp), const2)],
        out_specs=_route_out_specs(t, d, n_exp),
        out_shape=_route_out_shapes(b, l, d, n_exp),
        scratch_shapes=[pltpu.VMEM((1, n_exp), F32)],
        compiler_params=_cparams(("arbitrary", "arbitrary")),
        name="rg_out",
    )(x, mods, gg, hf, hb, w_out, rw, rb)


GU_ROWS = 512


def _gu_split_kernel(w_ref, o_ref):
    w = w_ref[0].astype(BF16)
    de = w.shape[1] // 2
    pair = 2 * LANES
    ri = lax.broadcasted_iota(jnp.int32, (pair, pair), 0)
    ci = lax.broadcasted_iota(jnp.int32, (pair, pair), 1)
    src = jnp.where(ci < LANES, 2 * ci, 2 * (ci - LANES) + 1)
    sel = jnp.where(ri == src, 1.0, 0.0).astype(BF16)
    for j in range(w.shape[1] // pair):
        r = jnp.dot(w[:, j * pair:(j + 1) * pair], sel, preferred_element_type=F32)
        o_ref[0, :, j * LANES:(j + 1) * LANES] = r[:, :LANES].astype(BF16)
        o_ref[0, :, de + j * LANES:de + (j + 1) * LANES] = r[:, LANES:].astype(BF16)


def _gu_split_call(w_gu):
    n_exp, d, de2 = w_gu.shape
    spec = pl.BlockSpec((1, GU_ROWS, de2), lambda e, r: (e, r, 0))
    return pl.pallas_call(
        _gu_split_kernel,
        grid=(n_exp, d // GU_ROWS),
        in_specs=[spec],
        out_specs=spec,
        out_shape=jax.ShapeDtypeStruct((n_exp, d, de2), BF16),
        compiler_params=_cparams(("parallel", "parallel")),
        name="moe_gu_split",
    )(w_gu)


def _dispatch_kernel(pend_ref, padded_ref, dest_ref, hp_ref, xs_ref, zero_sc, sem, zsem, *, n_blocks):
    step = pl.program_id(0)
    n_exp = pend_ref.shape[0]

    @pl.when(step == 0)
    def _():
        zero_sc[...] = jnp.zeros(zero_sc.shape, U32)

        def zero_block(row0):
            rows = pl.ds(pl.multiple_of(row0, MOE_BLK), MOE_BLK)
            return pltpu.make_async_copy(zero_sc, xs_ref.at[rows], zsem)

        def per_expert(e, carry):
            @pl.when(padded_ref[e] > 0)
            def _():
                cp = zero_block(pend_ref[e] - MOE_BLK)
                cp.start()
                cp.wait()
            return carry
        lax.fori_loop(0, n_exp, per_expert, 0)

        def per_tail(i, carry):
            cp = zero_block(i * MOE_BLK)
            cp.start()
            cp.wait()
            return carry
        lax.fori_loop(pend_ref[n_exp - 1] // MOE_BLK, n_blocks, per_tail, 0)

    def body(r, carry):
        for kk in range(TOP_K):
            d = dest_ref[r * TOP_K + kk]
            pltpu.make_async_copy(hp_ref.at[pl.ds(r, 1)], xs_ref.at[pl.ds(d, 1)], sem).start()
        return carry
    lax.fori_loop(0, TM, body, 0)
    for _ in range(TOP_K):
        pltpu.make_async_copy(hp_ref, xs_ref.at[pl.ds(0, TM)], sem).wait()


def _dispatch_call(pend, padded, dest_flat, hp, n_blocks):
    n, c = hp.shape
    kern = functools.partial(_dispatch_kernel, n_blocks=n_blocks)
    return pl.pallas_call(
        kern,
        grid_spec=pltpu.PrefetchScalarGridSpec(
            num_scalar_prefetch=2,
            grid=(n // TM,),
            in_specs=[pl.BlockSpec((TM * TOP_K,), lambda i, *_: (i,), memory_space=pltpu.SMEM),
                      pl.BlockSpec((TM, c), lambda i, *_: (i, 0))],
            out_specs=pl.BlockSpec(memory_space=pl.ANY),
            scratch_shapes=[pltpu.VMEM((MOE_BLK, c), U32),
                            pltpu.SemaphoreType.DMA(()),
                            pltpu.SemaphoreType.DMA(())]),
        out_shape=jax.ShapeDtypeStruct((n_blocks * MOE_BLK, c), U32),
        compiler_params=pltpu.CompilerParams(dimension_semantics=("arbitrary",),
                                             vmem_limit_bytes=VMEM_LIMIT, has_side_effects=True),
        name="moe_dispatch",
    )(pend, padded, dest_flat, hp)


def _expert_kernel(blk_e_ref, n_used_ref, xs_ref, wgu_ref, bgu_ref, wd_ref, bd_ref, ys_ref):
    used = pl.program_id(0) < n_used_ref[0]

    @pl.when(used)
    def _():
        w = xs_ref[...]
        x = jnp.concatenate([_unpack_hi(w).astype(BF16), _unpack_lo(w).astype(BF16)], axis=-1)
        gu = jnp.dot(x, wgu_ref[0], preferred_element_type=F32) + bgu_ref[0]
        de = gu.shape[1] // 2
        g = jnp.minimum(gu[:, :de], SWIGLU_LIMIT)
        u = jnp.clip(gu[:, de:], -SWIGLU_LIMIT, SWIGLU_LIMIT)
        act = (u + 1.0) * (g * _sigmoid(SWIGLU_ALPHA * g))
        y = jnp.dot(act.astype(BF16), wd_ref[0], preferred_element_type=F32) + bd_ref[0]
        ys_ref[...] = _pack_bf16_pair(y)

    @pl.when(jnp.logical_not(used))
    def _():
        ys_ref[...] = jnp.zeros(ys_ref.shape, U32)


def _expert_call(blk_e, n_used, xs, wgu, bgu, wd, bd):
    p_rows, c = xs.shape
    n_exp, d, de2 = wgu.shape
    de = de2 // 2
    by_e3 = lambda i, be, nu: (be[i], 0, 0)
    return pl.pallas_call(
        _expert_kernel,
        grid_spec=pltpu.PrefetchScalarGridSpec(
            num_scalar_prefetch=2,
            grid=(p_rows // MOE_BLK,),
            in_specs=[pl.BlockSpec((MOE_BLK, c), lambda i, be, nu: (i, 0)),
                      pl.BlockSpec((1, d, de2), by_e3),
                      pl.BlockSpec((1, 1, de2), by_e3),
                      pl.BlockSpec((1, de, d), by_e3),
                      pl.BlockSpec((1, 1, d), by_e3)],
            out_specs=pl.BlockSpec((MOE_BLK, c), lambda i, be, nu: (i, 0))),
        out_shape=jax.ShapeDtypeStruct((p_rows, c), U32),
        compiler_params=_cparams(("arbitrary",)),
        name="moe_experts",
    )(blk_e, n_used, xs, wgu, bgu, wd, bd)


def _combine_kernel(dest_ref, meta_ref, x_ref, mods_ref, ys_ref, o_ref, buf, sem):
    def body(r, carry):
        for kk in range(TOP_K):
            d = dest_ref[r * TOP_K + kk]
            pltpu.make_async_copy(ys_ref.at[pl.ds(d, 1)], buf.at[kk, pl.ds(r, 1)], sem).start()
        return carry
    lax.fori_loop(0, TM, body, 0)
    for kk in range(TOP_K):
        pltpu.make_async_copy(ys_ref.at[pl.ds(0, TM)], buf.at[kk], sem).wait()

    meta = meta_ref[...]
    c = buf.shape[-1]
    y_hi = jnp.zeros((TM, c), F32)
    y_lo = jnp.zeros((TM, c), F32)
    for kk in range(TOP_K):
        gk = meta[:, TOP_K + kk:TOP_K + kk + 1]
        w = buf[kk]
        y_hi = y_hi + gk * _unpack_hi(w)
        y_lo = y_lo + gk * _unpack_lo(w)
    gf = mods_ref[0, 0][5:6, :]
    o_ref[0, :, :c] = x_ref[0, :, :c] + gf[:, :c] * y_hi
    o_ref[0, :, c:] = x_ref[0, :, c:] + gf[:, c:] * y_lo


def _combine_call(dest_flat, meta, x, mods, ys):
    b, l, d = x.shape
    t = l // TM
    c = ys.shape[-1]
    row = lambda bb, tt: (bb, tt, 0)
    return pl.pallas_call(
        _combine_kernel,
        grid=(b, t),
        in_specs=[pl.BlockSpec((TM * TOP_K,), lambda bb, tt: (bb * t + tt,), memory_space=pltpu.SMEM),
                  pl.BlockSpec((TM, LANES), lambda bb, tt: (bb * t + tt, 0)),
                  pl.BlockSpec((1, TM, d), row),
                  _mods_spec(d),
                  pl.BlockSpec(memory_space=pl.ANY)],
        out_specs=pl.BlockSpec((1, TM, d), row),
        out_shape=jax.ShapeDtypeStruct((b, l, d), F32),
        scratch_shapes=[pltpu.VMEM((TOP_K, TM, c), U32), pltpu.SemaphoreType.DMA(())],
        compiler_params=_cparams(("arbitrary", "arbitrary")),
        name="moe_combine",
    )(dest_flat, meta, x, mods, ys)


def _moe(x_new, hp, meta, counts, mods, wgu, bgu, wd, bd):
    n = hp.shape[0]
    n_exp = wgu.shape[0]
    n_blocks = -(-n * TOP_K // MOE_BLK) + n_exp
    counts = counts.reshape(n_exp).astype(jnp.int32)
    padded = (counts + MOE_BLK - 1) // MOE_BLK * MOE_BLK
    pend = jnp.cumsum(padded).astype(jnp.int32)
    pstart = pend - padded
    top_e = meta[:, 0:TOP_K].astype(jnp.int32)
    rank = meta[:, 2 * TOP_K:3 * TOP_K].astype(jnp.int32)
    dest = (pstart[top_e] + rank).reshape(n * TOP_K)
    blk_row0 = jnp.arange(n_blocks, dtype=jnp.int32) * MOE_BLK
    blk_e = jnp.minimum(jnp.sum((pend[None, :] <= blk_row0[:, None]).astype(jnp.int32), axis=1),
                        n_exp - 1)
    xs = _dispatch_call(pend, padded, dest, hp, n_blocks)
    ys = _expert_call(blk_e, pend[n_exp - 1:] // MOE_BLK, xs, wgu, bgu, wd, bd)
    return _combine_call(dest, meta, x_new, mods, ys)


def _final_norm_kernel(x_ref, g_ref, o_ref):
    x = x_ref[0]
    o_ref[0] = x * lax.rsqrt(jnp.mean(x * x, axis=-1, keepdims=True) + EPS) * g_ref[...]


def _final_norm_call(x, gain, n_ctx):
    b, l, d = x.shape
    s = l - n_ctx
    skip = n_ctx // TM
    return pl.pallas_call(
        _final_norm_kernel,
        grid=(b, s // TM),
        in_specs=[pl.BlockSpec((1, TM, d), lambda bb, tt: (bb, tt + skip, 0)),
                  pl.BlockSpec((1, d), lambda bb, tt: (0, 0))],
        out_specs=pl.BlockSpec((1, TM, d), lambda bb, tt: (bb, tt, 0)),
        out_shape=jax.ShapeDtypeStruct((b, s, d), F32),
        compiler_params=_cparams(("parallel", "parallel")),
        name="final_norm",
    )(x, gain.reshape(1, d))


def _rope_tables(s, n_ctx):
    t = jnp.arange(s, dtype=jnp.int32)
    inv_freq = ROPE_THETA ** (-jnp.arange(ROPE_AXIS_DIM // 2, dtype=F32) * 2.0 / ROPE_AXIS_DIM)
    ang_r = (t // GRID_W).astype(F32)[:, None] * inv_freq
    ang_c = (t % GRID_W).astype(F32)[:, None] * inv_freq
    cr, sr, cc, sc = jnp.cos(ang_r), jnp.sin(ang_r), jnp.cos(ang_c), jnp.sin(ang_c)
    cos_t = jnp.concatenate([cr, cr, cc, cc], axis=-1)
    sin_t = jnp.concatenate([-sr, sr, -sc, sc], axis=-1)
    cos_t = jnp.concatenate([jnp.ones((n_ctx, HEAD_DIM), F32), cos_t], axis=0)
    sin_t = jnp.concatenate([jnp.zeros((n_ctx, HEAD_DIM), F32), sin_t], axis=0)
    return cos_t, sin_t


def kernel(x, c, ctx, c_ctx, mod_w, mod_b, ap_w_in, ap_q_gain, ap_k_gain, ap_pool_w, ap_pool_scale, ap_w_out, rg_w_in, rg_conv_w, rg_conv_b, rg_gate_r_w, rg_gate_r_b, rg_gate_i_w, rg_gate_i_b, rg_lambda, rg_w_out, moe_router_w, moe_router_b, moe_w_gu, moe_b_gu, moe_w_down, moe_b_down, final_norm_g):
    b, s, d = x.shape
    n_ctx = ctx.shape[1]
    depth = mod_w.shape[0]
    assert n_ctx % TM == 0 and s % TM == 0 and s % (2 * ATT_CK) == 0 and s % GRID_W == 0
    assert b + 1 <= SUBLANES and d % (2 * LANES) == 0

    cv = jnp.zeros((SUBLANES, d), F32).at[:b].set(c).at[b].set(c_ctx)
    mods_all = _mods_call(cv, mod_w, mod_b).reshape(depth, SUBLANES, 6, d)
    cos_t, sin_t = _rope_tables(s, n_ctx)

    xx = jnp.concatenate([ctx, x], axis=1)
    d_rnn = rg_w_out.shape[1]
    rnn_blk = rg_gate_r_w.shape[-1]
    assert d_rnn % SCAN_LANES == 0 and d_rnn % (3 * LANES) == 0

    for l in range(depth):
        i = l // 2
        ml = mods_all[l]
        mods = jnp.stack([jnp.broadcast_to(ml[b][None], (b, 6, d)), ml[:b]])
        rw_hi = moe_router_w[l].astype(BF16)
        rw = jnp.stack([rw_hi, (moe_router_w[l] - rw_hi.astype(F32)).astype(BF16)])
        rb = moe_router_b[l].reshape(1, -1)
        if l % 2 == 0:
            q, k, v, p = _attn_in_call(xx, mods, ap_w_in[i].astype(BF16),
                                       ap_q_gain[i].reshape(1, -1), ap_k_gain[i].reshape(1, -1),
                                       cos_t, sin_t)
            ao = _attention_call(q, k, v, n_ctx)
            wo = ap_w_out[i].astype(BF16)
            x_new, hp, meta, counts = _attn_out_call(
                xx, mods, ao, p, ap_pool_w[i].astype(BF16), ap_pool_scale[i].reshape(1, -1),
                wo[:ATTN_DIM], wo[ATTN_DIM:], rw, rb, n_ctx)
        else:
            w_in = rg_w_in[i].astype(BF16)
            gg = _rg_in_call(xx, mods, w_in[:, :d_rnn], True, BF16, "rg_in_gate")
            u = _rg_in_call(xx, mods, w_in[:, d_rnn:], False, F32, "rg_in_u")
            hs = []
            for dr in range(2):
                slabs = _gate_slabs(rg_gate_r_w[i, dr], rg_gate_i_w[i, dr], d_rnn)
                gate_b = jnp.stack([rg_gate_r_b[i, dr], rg_gate_i_b[i, dr]])
                hs.append(_rg_scan_call(u, rg_conv_w[i], rg_conv_b[i].reshape(1, -1), slabs, gate_b,
                                        rg_lambda[i, dr], dr == 1, rnn_blk))
            x_new, hp, meta, counts = _rg_out_call(xx, mods, gg, hs[0], hs[1],
                                                   rg_w_out[i].astype(BF16), rw, rb)
        de = moe_w_down.shape[2]
        bgu = jnp.concatenate([moe_b_gu[l][:, 0::2], moe_b_gu[l][:, 1::2]], axis=-1)
        xx = _moe(x_new, hp, meta, counts, mods, _gu_split_call(moe_w_gu[l]),
                  bgu.reshape(-1, 1, 2 * de),
                  moe_w_down[l].astype(BF16), moe_b_down[l].reshape(-1, 1, d))

    return _final_norm_call(xx, final_norm_g, n_ctx)
```

```python
import functools

import jax
import jax.numpy as jnp
import numpy as np
from jax import lax
from jax.experimental import pallas as pl
from jax.experimental.pallas import tpu as pltpu

F32 = jnp.float32
BF16 = jnp.bfloat16
U32 = jnp.uint32

EPS = 1e-6
GRID_W = 64
HEAD_DIM = 128
N_Q_HEADS = 12
N_KV_HEADS = 4
Q_PER_KV = N_Q_HEADS // N_KV_HEADS
ATTN_DIM = N_Q_HEADS * HEAD_DIM
KV_DIM = N_KV_HEADS * HEAD_DIM
ROPE_THETA = 10000.0
ROPE_AXIS_DIM = HEAD_DIM // 2
POOL_WINDOWS = (2, 4, 8, 16)
N_POOL_GROUPS = 4
N_RNN_BLOCKS = 16
CONV_WIDTH = 4
CONV_LEFT = 2
RG_C = 8.0
TOP_K = 4
SWIGLU_LIMIT = 7.0
SWIGLU_ALPHA = 1.702

LANES = 128
SUBLANES = 8
MXU_DIM = 256
VMEM_LIMIT = 56 * 1024 * 1024

TM = 256
HALO = SUBLANES
ATT_CK = 1024
LOG2E = 1.4426950408889634
MOE_BLK = 512
GATE_TN = 256
SCAN_LANES = 896


def _cparams(sem):
    return pltpu.CompilerParams(dimension_semantics=sem, vmem_limit_bytes=VMEM_LIMIT)


def _sigmoid(x):
    return 0.5 * jnp.tanh(0.5 * x) + 0.5


def _norm_mod(x, mods, i_shift, i_scale):
    ms = jnp.mean(x * x, axis=-1, keepdims=True)
    xn = x * lax.rsqrt(ms + EPS)
    return xn * (1.0 + mods[i_scale:i_scale + 1, :]) + mods[i_shift:i_shift + 1, :]


def _pack_bf16_pair(h):
    c = h.shape[-1] // 2
    bits = pltpu.bitcast(h.astype(BF16).astype(F32), U32)
    return (bits[:, :c] & jnp.uint32(0xFFFF0000)) | (bits[:, c:] >> jnp.uint32(16))


def _unpack_hi(w):
    return pltpu.bitcast(w & jnp.uint32(0xFFFF0000), F32)


def _unpack_lo(w):
    return pltpu.bitcast(w << jnp.uint32(16), F32)


def _mods_kernel(cv_ref, w_ref, b_ref, o_ref):
    cv = cv_ref[...]
    s = cv * _sigmoid(cv)
    o_ref[0] = jnp.dot(s.astype(BF16), w_ref[0].astype(BF16), preferred_element_type=F32) + b_ref[0]


def _mods_call(cv, mod_w, mod_b):
    depth, d, n6 = mod_w.shape
    tn = 1536
    return pl.pallas_call(
        _mods_kernel,
        grid=(depth, n6 // tn),
        in_specs=[pl.BlockSpec((SUBLANES, d), lambda l, j: (0, 0)),
                  pl.BlockSpec((1, d, tn), lambda l, j: (l, 0, j)),
                  pl.BlockSpec((1, 1, tn), lambda l, j: (l, 0, j))],
        out_specs=pl.BlockSpec((1, SUBLANES, tn), lambda l, j: (l, 0, j)),
        out_shape=jax.ShapeDtypeStruct((depth, SUBLANES, n6), F32),
        compiler_params=_cparams(("parallel", "parallel")),
        name="adaln_mods",
    )(cv, mod_w, mod_b.reshape(depth, 1, n6))


def _mods_spec(d):
    return pl.BlockSpec((1, 1, 6, d), lambda b, t: (jnp.minimum(t, 1), b, 0, 0))


def _attn_in_kernel(x_ref, mods_ref, w_ref, qg_ref, kg_ref, cos_ref, sin_ref,
                    q_ref, k_ref, v_ref, p_ref):
    h = _norm_mod(x_ref[0], mods_ref[0, 0], 0, 1).astype(BF16)
    z = jnp.dot(h, w_ref[...], preferred_element_type=F32)
    cos = cos_ref[...]
    sin = sin_ref[...]
    lane = lax.broadcasted_iota(jnp.int32, (TM, HEAD_DIM), 1)
    first = (lane % (ROPE_AXIS_DIM)) < (ROPE_AXIS_DIM // 2)

    def head(zh, gain, scale):
        n = zh * lax.rsqrt(jnp.mean(zh * zh, axis=-1, keepdims=True) + EPS) * gain
        half = ROPE_AXIS_DIM // 2
        swapped = jnp.where(first, pltpu.roll(n, HEAD_DIM - half, 1), pltpu.roll(n, half, 1))
        return (n * cos + swapped * sin) * scale

    qg = qg_ref[...]
    kg = kg_ref[...]
    for hh in range(N_Q_HEADS):
        sl = slice(hh * HEAD_DIM, (hh + 1) * HEAD_DIM)
        q_ref[0, :, sl] = head(z[:, sl], qg, LOG2E * HEAD_DIM ** -0.5).astype(BF16)
    for hh in range(N_KV_HEADS):
        zs = slice(ATTN_DIM + hh * HEAD_DIM, ATTN_DIM + (hh + 1) * HEAD_DIM)
        k_ref[0, hh] = head(z[:, zs], kg, 1.0).T.astype(BF16)
    v_ref[0] = z[:, ATTN_DIM + KV_DIM:ATTN_DIM + 2 * KV_DIM].astype(BF16)
    p_ref[0] = z[:, ATTN_DIM + 2 * KV_DIM:]


def _attn_in_call(x, mods, w_in, q_gain, k_gain, cos_t, sin_t):
    b, l, d = x.shape
    t = l // TM
    n_in = w_in.shape[1]
    pool_dim = n_in - ATTN_DIM - 2 * KV_DIM
    row = lambda bb, tt: (bb, tt, 0)
    return pl.pallas_call(
        _attn_in_kernel,
        grid=(b, t),
        in_specs=[pl.BlockSpec((1, TM, d), row),
                  _mods_spec(d),
                  pl.BlockSpec((d, n_in), lambda bb, tt: (0, 0)),
                  pl.BlockSpec((1, HEAD_DIM), lambda bb, tt: (0, 0)),
                  pl.BlockSpec((1, HEAD_DIM), lambda bb, tt: (0, 0)),
                  pl.BlockSpec((TM, HEAD_DIM), lambda bb, tt: (tt, 0)),
                  pl.BlockSpec((TM, HEAD_DIM), lambda bb, tt: (tt, 0))],
        out_specs=[pl.BlockSpec((1, TM, ATTN_DIM), row),
                   pl.BlockSpec((1, N_KV_HEADS, HEAD_DIM, TM), lambda bb, tt: (bb, 0, 0, tt)),
                   pl.BlockSpec((1, TM, KV_DIM), row),
                   pl.BlockSpec((1, TM, pool_dim), row)],
        out_shape=[jax.ShapeDtypeStruct((b, l, ATTN_DIM), BF16),
                   jax.ShapeDtypeStruct((b, N_KV_HEADS, HEAD_DIM, l), BF16),
                   jax.ShapeDtypeStruct((b, l, KV_DIM), BF16),
                   jax.ShapeDtypeStruct((b, l, pool_dim), F32)],
        compiler_params=_cparams(("parallel", "parallel")),
        name="attn_in",
    )(x, mods, w_in, q_gain, k_gain, cos_t, sin_t)


def _attention_kernel(q_ref, kt_ref, v_ref, o_ref, sa_sc, sb_sc, m_sc, acc_sc, *, n_ctx, n_lat_chunks):
    t = pl.program_id(2)
    q = jnp.concatenate([q_ref[0, :, g * HEAD_DIM:(g + 1) * HEAD_DIM] for g in range(Q_PER_KV)],
                        axis=0)
    m_sc[...] = jnp.full(m_sc.shape, -1e30, F32)
    acc_sc[...] = jnp.zeros(acc_sc.shape, F32)

    def scores(start, size):
        return jnp.dot(q, kt_ref[0, 0, :, pl.ds(start, size)], preferred_element_type=F32)

    def v_ext(start, size):
        return jnp.concatenate([v_ref[0, pl.ds(start, size), :], jnp.ones((size, HEAD_DIM), BF16)],
                               axis=-1)

    def softmax_pv(load_s, vx, width):
        for g in range(Q_PER_KV):
            rows = slice(g * TM, (g + 1) * TM)
            s = load_s(rows)
            blocks = [s[:, c * LANES:(c + 1) * LANES] for c in range(width // LANES)]
            mc = blocks[0]
            for blk in blocks[1:]:
                mc = jnp.maximum(mc, blk)
            m_prev = m_sc[rows, :]
            m_new = jnp.maximum(m_prev, jnp.broadcast_to(jnp.max(mc, axis=-1, keepdims=True),
                                                         (TM, LANES)))
            alpha = jnp.exp2(m_prev - m_new)
            p = jnp.concatenate([jnp.exp2(blk - m_new).astype(BF16) for blk in blocks], axis=-1)
            pv = jnp.dot(p, vx, preferred_element_type=F32)
            acc_sc[rows, :HEAD_DIM] = alpha * acc_sc[rows, :HEAD_DIM] + pv[:, :HEAD_DIM]
            acc_sc[rows, HEAD_DIM:] = alpha * acc_sc[rows, HEAD_DIM:] + pv[:, HEAD_DIM:]
            m_sc[rows, :] = m_new

    s0 = scores(0, n_ctx)
    softmax_pv(lambda rows: s0[rows], v_ext(0, n_ctx), n_ctx)

    @pl.when(t > 0)
    def _():
        def start(c):
            return pl.multiple_of(n_ctx + c * ATT_CK, LANES)

        def pair(c, prefetch):
            sb_sc[...] = scores(start(c + 1), ATT_CK)
            softmax_pv(lambda rows: sa_sc[rows, :], v_ext(start(c), ATT_CK), ATT_CK)
            if prefetch:
                sa_sc[...] = scores(start(c + 2), ATT_CK)
            softmax_pv(lambda rows: sb_sc[rows, :], v_ext(start(c + 1), ATT_CK), ATT_CK)

        sa_sc[...] = scores(n_ctx, ATT_CK)

        def body(i, carry):
            pair(2 * i, True)
            return carry
        lax.fori_loop(0, n_lat_chunks // 2 - 1, body, 0)
        pair(n_lat_chunks - 2, False)

    for g in range(Q_PER_KV):
        rows = slice(g * TM, (g + 1) * TM)
        o = acc_sc[rows, :HEAD_DIM] / acc_sc[rows, HEAD_DIM:]
        o_ref[0, :, g * HEAD_DIM:(g + 1) * HEAD_DIM] = o.astype(BF16)


def _attention_call(q, k, v, n_ctx):
    b, l, _ = q.shape
    t = l // TM
    gw = Q_PER_KV * HEAD_DIM
    kern = functools.partial(_attention_kernel, n_ctx=n_ctx, n_lat_chunks=(l - n_ctx) // ATT_CK)
    return pl.pallas_call(
        kern,
        grid=(b, N_KV_HEADS, t),
        in_specs=[pl.BlockSpec((1, TM, gw), lambda bb, hh, tt: (bb, tt, hh)),
                  pl.BlockSpec((1, 1, HEAD_DIM, l), lambda bb, hh, tt: (bb, hh, 0, 0)),
                  pl.BlockSpec((1, l, HEAD_DIM), lambda bb, hh, tt: (bb, 0, hh))],
        out_specs=pl.BlockSpec((1, TM, gw), lambda bb, hh, tt: (bb, tt, hh)),
        out_shape=jax.ShapeDtypeStruct((b, l, ATTN_DIM), BF16),
        scratch_shapes=[pltpu.VMEM((Q_PER_KV * TM, ATT_CK), F32),
                        pltpu.VMEM((Q_PER_KV * TM, ATT_CK), F32),
                        pltpu.VMEM((Q_PER_KV * TM, LANES), F32),
                        pltpu.VMEM((Q_PER_KV * TM, 2 * HEAD_DIM), F32)],
        compiler_params=_cparams(("parallel", "parallel", "parallel")),
        name="attention",
    )(q, k, v)


def _residual_and_route(x, mix, mods, rw_ref, rb_ref, cnt_sc, xo_ref, hp_ref, meta_ref):
    first_step = (pl.program_id(0) == 0) & (pl.program_id(1) == 0)

    @pl.when(first_step)
    def _():
        cnt_sc[...] = jnp.zeros(cnt_sc.shape, F32)

    xnew = x + mods[2:3, :] * mix
    xo_ref[0] = xnew
    h2 = _norm_mod(xnew, mods, 3, 4)
    hp_ref[...] = _pack_bf16_pair(h2)

    n_exp = rw_ref.shape[-1]
    h_hi = h2.astype(BF16)
    h_lo = (h2 - h_hi.astype(F32)).astype(BF16)
    logits = (jnp.dot(h_hi, rw_ref[0], preferred_element_type=F32)
              + jnp.dot(h_lo, rw_ref[0], preferred_element_type=F32)
              + jnp.dot(h_hi, rw_ref[1], preferred_element_type=F32)) + rb_ref[...]
    lane = lax.broadcasted_iota(jnp.int32, (TM, n_exp), 1).astype(F32)
    onehots, tops, idxs = [], [], []
    cur = logits
    for _ in range(TOP_K):
        m = jnp.max(cur, axis=-1, keepdims=True)
        idx = jnp.min(jnp.where(cur == m, lane, float(n_exp)), axis=-1, keepdims=True)
        sel = lane == idx
        tops.append(m)
        idxs.append(idx)
        onehots.append(sel)
        cur = jnp.where(sel, -jnp.inf, cur)
    ws = [jnp.exp(m - tops[0]) for m in tops]
    denom = ws[0] + ws[1] + ws[2] + ws[3]

    counts = jnp.zeros((TM, n_exp), F32)
    for sel in onehots:
        counts = counts + jnp.where(sel, 1.0, 0.0)
    ri = lax.broadcasted_iota(jnp.int32, (TM, TM), 0)
    ci = lax.broadcasted_iota(jnp.int32, (TM, TM), 1)
    tri = jnp.where(ci < ri, 1.0, 0.0).astype(BF16)
    before = jnp.dot(tri, counts.astype(BF16), preferred_element_type=F32) + cnt_sc[...]

    lane_m = lax.broadcasted_iota(jnp.int32, (TM, LANES), 1)
    meta = jnp.zeros((TM, LANES), F32)
    for kk in range(TOP_K):
        rank = jnp.sum(jnp.where(onehots[kk], before, 0.0), axis=-1, keepdims=True)
        meta = jnp.where(lane_m == kk, idxs[kk], meta)
        meta = jnp.where(lane_m == TOP_K + kk, ws[kk] / denom, meta)
        meta = jnp.where(lane_m == 2 * TOP_K + kk, rank, meta)
    meta_ref[...] = meta
    cnt_sc[...] = cnt_sc[...] + jnp.sum(counts, axis=0, keepdims=True)
    return cnt_sc[...]


def _route_out_specs(t, d, n_exp):
    flat = lambda bb, tt: (bb * t + tt, 0)
    return [pl.BlockSpec((1, TM, d), lambda bb, tt: (bb, tt, 0)),
            pl.BlockSpec((TM, d // 2), flat),
            pl.BlockSpec((TM, LANES), flat),
            pl.BlockSpec((1, n_exp), lambda bb, tt: (0, 0))]


def _route_out_shapes(b, l, d, n_exp):
    return [jax.ShapeDtypeStruct((b, l, d), F32),
            jax.ShapeDtypeStruct((b * l, d // 2), U32),
            jax.ShapeDtypeStruct((b * l, LANES), F32),
            jax.ShapeDtypeStruct((1, n_exp), F32)]


def _attn_out_kernel(x_ref, mods_ref, ao_ref, p_ref, pp_ref, pn_ref, pw_ref, ps_ref,
                     woa_ref, wop_ref, rw_ref, rb_ref,
                     xo_ref, hp_ref, meta_ref, cnt_ref, e_sc, cnt_sc, *, n_ctx, n_lat):
    t = pl.program_id(1)
    nt = pl.num_programs(1)
    prev_ok = t >= 2
    next_ok = (t >= 1) & (t <= nt - 2)
    e_sc[0:HALO, :] = jnp.where(prev_ok, pp_ref[0], 0.0)
    e_sc[HALO:HALO + TM, :] = p_ref[0]
    e_sc[HALO + TM:, :] = jnp.where(next_ok, pn_ref[0], 0.0)

    rowi = lax.broadcasted_iota(jnp.int32, (TM, 1), 0)
    pos = jnp.where(t == 0, rowi, (t - 1) * TM + rowi)
    seq_len = jnp.where(t == 0, n_ctx, n_lat)
    gw = p_ref.shape[-1] // N_POOL_GROUPS
    mix = jnp.dot(ao_ref[0], woa_ref[...], preferred_element_type=F32)
    for g, w in enumerate(POOL_WINDOWS):
        ls = slice(g * gw, (g + 1) * gw)
        acc = e_sc[HALO - w // 2:HALO - w // 2 + TM, ls]
        for j in range(1, w):
            off = HALO - w // 2 + j
            acc = acc + e_sc[off:off + TM, ls]
        lo = jnp.maximum(pos - w // 2, 0)
        hi = jnp.minimum(pos + w - w // 2, seq_len)
        dlt = acc / (hi - lo).astype(F32) - e_sc[HALO:HALO + TM, ls]
        y = jnp.dot(dlt.astype(BF16), pw_ref[g], preferred_element_type=F32) * ps_ref[:, ls]
        mix = mix + jnp.dot(y.astype(BF16), wop_ref[ls, :], preferred_element_type=F32)

    cnt_ref[...] = _residual_and_route(x_ref[0], mix, mods_ref[0, 0], rw_ref, rb_ref, cnt_sc,
                                       xo_ref, hp_ref, meta_ref)


def _attn_out_call(x, mods, ao, p, pool_w, pool_scale, wo_a, wo_p, rw, rb, n_ctx):
    b, l, d = x.shape
    t = l // TM
    pool_dim = p.shape[-1]
    n_exp = rw.shape[-1]
    nh = l // HALO
    per = TM // HALO
    row = lambda bb, tt: (bb, tt, 0)
    const2 = lambda bb, tt: (0, 0)
    kern = functools.partial(_attn_out_kernel, n_ctx=n_ctx, n_lat=l - n_ctx)
    return pl.pallas_call(
        kern,
        grid=(b, t),
        in_specs=[pl.BlockSpec((1, TM, d), row),
                  _mods_spec(d),
                  pl.BlockSpec((1, TM, ATTN_DIM), row),
                  pl.BlockSpec((1, TM, pool_dim), row),
                  pl.BlockSpec((1, HALO, pool_dim), lambda bb, tt: (bb, jnp.maximum(tt * per - 1, 0), 0)),
                  pl.BlockSpec((1, HALO, pool_dim), lambda bb, tt: (bb, jnp.minimum((tt + 1) * per, nh - 1), 0)),
                  pl.BlockSpec(pool_w.shape, lambda bb, tt: (0, 0, 0)),
                  pl.BlockSpec((1, pool_dim), const2),
                  pl.BlockSpec(wo_a.shape, const2),
                  pl.BlockSpec(wo_p.shape, const2),
                  pl.BlockSpec(rw.shape, lambda bb, tt: (0, 0, 0)),
                  pl.BlockSpec((1, n_exp), const2)],
        out_specs=_route_out_specs(t, d, n_exp),
        out_shape=_route_out_shapes(b, l, d, n_exp),
        scratch_shapes=[pltpu.VMEM((TM + 2 * HALO, pool_dim), F32),
                        pltpu.VMEM((1, n_exp), F32)],
        compiler_params=_cparams(("arbitrary", "arbitrary")),
        name="attn_out",
    )(x, mods, ao, p, p, p, pool_w, pool_scale, wo_a, wo_p, rw, rb)


def _gelu_tanh(x):
    return 0.5 * x * (1.0 + jnp.tanh(0.7978845608028654 * (x + 0.044715 * (x * x * x))))


SEG_LEN = TM // SUBLANES


def _segment_perm(to_segment_major):
    ri = lax.broadcasted_iota(jnp.int32, (TM, TM), 0)
    ci = lax.broadcasted_iota(jnp.int32, (TM, TM), 1)
    if to_segment_major:
        hit = ci == (ri % SUBLANES) * SEG_LEN + ri // SUBLANES
    else:
        hit = ci == (ri % SEG_LEN) * SUBLANES + ri // SEG_LEN
    return jnp.where(hit, 1.0, 0.0).astype(BF16)


def _rg_in_kernel(x_ref, mods_ref, w_ref, o_ref, *, gelu):
    h = _norm_mod(x_ref[0], mods_ref[0, 0], 0, 1).astype(BF16)
    h = jnp.dot(_segment_perm(True), h, preferred_element_type=F32).astype(BF16)
    z = jnp.dot(h, w_ref[...], preferred_element_type=F32)
    if gelu:
        z = _gelu_tanh(z)
    o_ref[0] = z.astype(o_ref.dtype)


def _rg_in_call(x, mods, w, gelu, out_dtype, name):
    b, l, d = x.shape
    t = l // TM
    n = w.shape[1]
    row = lambda bb, tt: (bb, tt, 0)
    return pl.pallas_call(
        functools.partial(_rg_in_kernel, gelu=gelu),
        grid=(b, t),
        in_specs=[pl.BlockSpec((1, TM, d), row), _mods_spec(d),
                  pl.BlockSpec((d, n), lambda bb, tt: (0, 0))],
        out_specs=pl.BlockSpec((1, TM, n), row),
        out_shape=jax.ShapeDtypeStruct((b, l, n), out_dtype),
        compiler_params=_cparams(("parallel", "parallel")),
        name=name,
    )(x, mods, w)


def _gate_slab_starts(d_rnn, blk):
    n_slabs = -(-d_rnn // GATE_TN)
    spans = []
    for j in range(n_slabs):
        c0, c1 = j * GATE_TN, min((j + 1) * GATE_TN, d_rnn)
        lo = (c0 // blk) * blk
        hi = ((c1 - 1) // blk + 1) * blk
        spans.append((lo // LANES * LANES, -(-hi // LANES) * LANES))
    km = max(h - l for l, h in spans)
    starts = [min(l, d_rnn - km) for l, _ in spans]
    return starts, km


def _gate_slabs(r_w, i_w, d_rnn):
    blk = r_w.shape[-1]
    starts, km = _gate_slab_starts(d_rnn, blk)
    n_slabs = len(starts)
    pad = n_slabs * GATE_TN - d_rnn
    eye = jnp.eye(r_w.shape[0], dtype=r_w.dtype)
    dense = [jnp.pad((w[:, :, None, :] * eye[:, None, :, None]).reshape(d_rnn, d_rnn),
                     ((0, 0), (0, pad))) for w in (r_w, i_w)]
    slabs = [jnp.concatenate([dd[s:s + km, j * GATE_TN:(j + 1) * GATE_TN] for dd in dense], axis=1)
             for j, s in enumerate(starts)]
    return jnp.stack(slabs).astype(BF16)


def _rg_scan_kernel(u_ref, up_ref, un_ref, cw_ref, cb_ref, w_ref, gb_ref, lam_ref, h_ref,
                    e_sc, ucb_sc, uc_sc, a_sc, b_sc, carry_sc, *, reverse, starts, km):
    s = pl.program_id(1)
    nt = pl.num_programs(1)
    tau = jnp.where(s == 0, 0, nt - s) if reverse else s
    d_rnn = u_ref.shape[-1]

    @pl.when(s == 0)
    def _():
        carry_sc[...] = jnp.zeros(carry_sc.shape, F32)

    prev_ok = tau >= 2
    next_ok = (tau >= 1) & (tau <= nt - 2)
    G = SUBLANES
    sub = lax.broadcasted_iota(jnp.int32, (G, d_rnn), 0)
    for j in range(CONV_LEFT):
        src = u_ref[0, TM - (CONV_LEFT - j) * G:TM - (CONV_LEFT - j - 1) * G, :]
        edge = jnp.where(prev_ok, up_ref[0, (j + 1) * G - 1:(j + 1) * G, :], 0.0)
        e_sc[j * G:(j + 1) * G, :] = jnp.where(sub == 0, edge, pltpu.roll(src, 1, 0))
    e_sc[CONV_LEFT * G:CONV_LEFT * G + TM, :] = u_ref[0]
    edge = jnp.where(next_ok, un_ref[0, 0:1, :], 0.0)
    e_sc[CONV_LEFT * G + TM:, :] = jnp.where(sub == G - 1, edge, pltpu.roll(u_ref[0, 0:G, :], G - 1, 0))
    cw = cw_ref[...]
    n_conv_chunks = d_rnn // (3 * LANES)
    for c in range(n_conv_chunks):
        ls = slice(c * 3 * LANES, (c + 1) * 3 * LANES)
        uc = cb_ref[:, ls] + cw[0:1, ls] * e_sc[0:TM, ls]
        for j in range(1, CONV_WIDTH):
            uc = uc + cw[j:j + 1, ls] * e_sc[j * G:j * G + TM, ls]
        uc_sc[:, ls] = uc
        ucb_sc[:, ls] = uc.astype(BF16)

    lam = lam_ref[0]
    neg = -lam
    softplus = jnp.maximum(neg, 0.0) + jnp.log(1.0 + jnp.exp(-jnp.abs(neg)))
    gb = gb_ref[0]
    for j, st in enumerate(starts):
        n = min(GATE_TN, d_rnn - j * GATE_TN)
        ls = slice(j * GATE_TN, j * GATE_TN + n)
        z = jnp.dot(ucb_sc[:, st:st + km], w_ref[0, j], preferred_element_type=F32)
        r = _sigmoid(z[:, :n] + gb[0:1, ls])
        i = _sigmoid(z[:, GATE_TN:GATE_TN + n] + gb[1:2, ls])
        log_a = (-RG_C) * r * softplus[:, ls]
        a = jnp.exp(log_a)
        mult = jnp.sqrt(jnp.maximum(1.0 - a * a, 0.0))
        b_sc[:, ls] = mult * (i * uc_sc[:, ls])
        a_sc[:, ls] = a

    steps = range(SEG_LEN - 1, -1, -1) if reverse else range(SEG_LEN)
    rowi = lax.broadcasted_iota(jnp.int32, (SUBLANES, SCAN_LANES), 0)
    grp = lambda t: slice(t * SUBLANES, (t + 1) * SUBLANES)
    for c in range(d_rnn // SCAN_LANES):
        ls = slice(c * SCAN_LANES, (c + 1) * SCAN_LANES)
        tot_h = jnp.zeros((SUBLANES, SCAN_LANES), F32)
        tot_p = jnp.ones((SUBLANES, SCAN_LANES), F32)
        for t in steps:
            a = a_sc[grp(t), ls]
            tot_h = a * tot_h + b_sc[grp(t), ls]
            tot_p = a * tot_p

        for sft in (1, 2, 4):
            if reverse:
                ok = rowi < SUBLANES - sft
                shift = SUBLANES - sft
            else:
                ok = rowi >= sft
                shift = sft
            pp = jnp.where(ok, pltpu.roll(tot_p, shift, 0), 1.0)
            hp = jnp.where(ok, pltpu.roll(tot_h, shift, 0), 0.0)
            tot_h = tot_p * hp + tot_h
            tot_p = tot_p * pp
        carry = carry_sc[:, ls]
        after = tot_p * carry + tot_h
        if reverse:
            h = jnp.where(rowi < SUBLANES - 1, pltpu.roll(after, SUBLANES - 1, 0), carry)
            carry_sc[:, ls] = jnp.broadcast_to(after[0:1, :], (SUBLANES, SCAN_LANES))
        else:
            h = jnp.where(rowi >= 1, pltpu.roll(after, 1, 0), carry)
            carry_sc[:, ls] = jnp.broadcast_to(after[SUBLANES - 1:SUBLANES, :], (SUBLANES, SCAN_LANES))

        for t in steps:
            h = a_sc[grp(t), ls] * h + b_sc[grp(t), ls]
            b_sc[grp(t), ls] = h

    h_ref[0] = b_sc[...].astype(h_ref.dtype)


def _rg_scan_call(u, conv_w, conv_b, slabs, gate_b, lam, reverse, blk):
    b, l, d_rnn = u.shape
    t = l // TM
    starts, km = _gate_slab_starts(d_rnn, blk)
    lead = CONV_LEFT * SUBLANES
    per_lead = TM // lead
    per_tail = TM // SUBLANES

    def tile(ss):
        return jnp.where(ss == 0, 0, t - ss) if reverse else ss

    kern = functools.partial(_rg_scan_kernel, reverse=reverse, starts=tuple(starts), km=km)
    const2 = lambda bb, ss: (0, 0)
    return pl.pallas_call(
        kern,
        grid=(b, t),
        in_specs=[pl.BlockSpec((1, TM, d_rnn), lambda bb, ss: (bb, tile(ss), 0)),
                  pl.BlockSpec((1, lead, d_rnn),
                               lambda bb, ss: (bb, jnp.maximum(tile(ss) * per_lead - 1, 0), 0)),
                  pl.BlockSpec((1, SUBLANES, d_rnn),
                               lambda bb, ss: (bb, jnp.minimum((tile(ss) + 1) * per_tail,
                                                               l // SUBLANES - 1), 0)),
                  pl.BlockSpec(conv_w.shape, const2),
                  pl.BlockSpec((1, d_rnn), const2),
                  pl.BlockSpec((1,) + slabs.shape, lambda bb, ss: (0, 0, 0, 0)),
                  pl.BlockSpec((1, 2, d_rnn), lambda bb, ss: (0, 0, 0)),
                  pl.BlockSpec((1, 1, d_rnn), lambda bb, ss: (0, 0, 0))],
        out_specs=pl.BlockSpec((1, TM, d_rnn), lambda bb, ss: (bb, tile(ss), 0)),
        out_shape=jax.ShapeDtypeStruct((b, l, d_rnn), BF16),
        scratch_shapes=[pltpu.VMEM((TM + (CONV_WIDTH - 1) * SUBLANES, d_rnn), F32),
                        pltpu.VMEM((TM, d_rnn), BF16),
                        pltpu.VMEM((TM, d_rnn), F32),
                        pltpu.VMEM((TM, d_rnn), F32),
                        pltpu.VMEM((TM, d_rnn), F32),
                        pltpu.VMEM((SUBLANES, d_rnn), F32)],
        compiler_params=_cparams(("arbitrary", "arbitrary")),
        name="rg_scan_bwd" if reverse else "rg_scan_fwd",
    )(u, u, u, conv_w, conv_b, slabs[None], gate_b[None], lam[None, None])


def _rg_out_kernel(x_ref, mods_ref, g_ref, hf_ref, hb_ref, wo_ref, rw_ref, rb_ref,
                   xo_ref, hp_ref, meta_ref, cnt_ref, cnt_sc):
    gated = g_ref[0].astype(F32) * (hf_ref[0].astype(F32) + hb_ref[0].astype(F32))
    gated = jnp.dot(_segment_perm(False), gated.astype(BF16), preferred_element_type=F32)
    mix = jnp.dot(gated.astype(BF16), wo_ref[...], preferred_element_type=F32)
    cnt_ref[...] = _residual_and_route(x_ref[0], mix, mods_ref[0, 0], rw_ref, rb_ref, cnt_sc,
                                       xo_ref, hp_ref, meta_ref)


def _rg_out_call(x, mods, gg, hf, hb, w_out, rw, rb):
    b, l, d = x.shape
    t = l // TM
    d_rnn = gg.shape[-1]
    n_exp = rw.shape[-1]
    row = lambda bb, tt: (bb, tt, 0)
    const2 = lambda bb, tt: (0, 0)
    return pl.pallas_call(
        _rg_out_kernel,
        grid=(b, t),
        in_specs=[pl.BlockSpec((1, TM, d), row), _mods_spec(d),
                  pl.BlockSpec((1, TM, d_rnn), row),
                  pl.BlockSpec((1, TM, d_rnn), row),
                  pl.BlockSpec((1, TM, d_rnn), row),
                  pl.BlockSpec(w_out.shape, const2),
                  pl.BlockSpec(rw.shape, lambda bb, tt: (0, 0, 0)),
                  pl.BlockSpec((1, n_exp), const2)],
        out_specs=_route_out_specs(t, d, n_exp),
        out_shape=_route_out_shapes(b, l, d, n_exp),
        scratch_shapes=[pltpu.VMEM((1, n_exp), F32)],
        compiler_params=_cparams(("arbitrary", "arbitrary")),
        name="rg_out",
    )(x, mods, gg, hf, hb, w_out, rw, rb)


GU_ROWS = 512


def _gu_split_kernel(w_ref, o_ref):
    w = w_ref[0, 0].astype(BF16)
    de = w.shape[1] // 2
    pair = 2 * LANES
    ri = lax.broadcasted_iota(jnp.int32, (pair, pair), 0)
    ci = lax.broadcasted_iota(jnp.int32, (pair, pair), 1)
    src = jnp.where(ci < LANES, 2 * ci, 2 * (ci - LANES) + 1)
    sel = jnp.where(ri == src, 1.0, 0.0).astype(BF16)
    for j in range(w.shape[1] // pair):
        r = jnp.dot(w[:, j * pair:(j + 1) * pair], sel, preferred_element_type=F32)
        o_ref[0, :, j * LANES:(j + 1) * LANES] = r[:, :LANES].astype(BF16)
        o_ref[0, :, de + j * LANES:de + (j + 1) * LANES] = r[:, LANES:].astype(BF16)


def _gu_split_call(w_gu_all, layer):
    _, n_exp, d, de2 = w_gu_all.shape
    return pl.pallas_call(
        _gu_split_kernel,
        grid=(n_exp, d // GU_ROWS),
        in_specs=[pl.BlockSpec((1, 1, GU_ROWS, de2), lambda e, r: (layer, e, r, 0))],
        out_specs=pl.BlockSpec((1, GU_ROWS, de2), lambda e, r: (e, r, 0)),
        out_shape=jax.ShapeDtypeStruct((n_exp, d, de2), BF16),
        compiler_params=_cparams(("parallel", "parallel")),
        name="moe_gu_split",
    )(w_gu_all)


def _dispatch_kernel(pend_ref, padded_ref, dest_ref, hp_ref, xs_ref, zero_sc, sem, zsem, *, n_blocks):
    step = pl.program_id(0)
    n_exp = pend_ref.shape[0]

    @pl.when(step == 0)
    def _():
        zero_sc[...] = jnp.zeros(zero_sc.shape, U32)

        def zero_block(row0):
            rows = pl.ds(pl.multiple_of(row0, MOE_BLK), MOE_BLK)
            return pltpu.make_async_copy(zero_sc, xs_ref.at[rows], zsem)

        first_tail = pend_ref[n_exp - 1] // MOE_BLK
        for go in (lambda cp: cp.start(), lambda cp: cp.wait()):
            def per_expert(e, carry, go=go):
                @pl.when(padded_ref[e] > 0)
                def _():
                    go(zero_block(pend_ref[e] - MOE_BLK))
                return carry
            lax.fori_loop(0, n_exp, per_expert, 0)

            def per_tail(i, carry, go=go):
                go(zero_block(i * MOE_BLK))
                return carry
            lax.fori_loop(first_tail, n_blocks, per_tail, 0)

    def body(r, carry):
        for kk in range(TOP_K):
            d = dest_ref[r * TOP_K + kk]
            pltpu.make_async_copy(hp_ref.at[pl.ds(r, 1)], xs_ref.at[pl.ds(d, 1)], sem).start()
        return carry
    lax.fori_loop(0, TM, body, 0)
    for _ in range(TOP_K):
        pltpu.make_async_copy(hp_ref, xs_ref.at[pl.ds(0, TM)], sem).wait()


def _dispatch_call(pend, padded, dest_flat, hp, n_blocks):
    n, c = hp.shape
    kern = functools.partial(_dispatch_kernel, n_blocks=n_blocks)
    return pl.pallas_call(
        kern,
        grid_spec=pltpu.PrefetchScalarGridSpec(
            num_scalar_prefetch=2,
            grid=(n // TM,),
            in_specs=[pl.BlockSpec((TM * TOP_K,), lambda i, *_: (i,), memory_space=pltpu.SMEM),
                      pl.BlockSpec((TM, c), lambda i, *_: (i, 0))],
            out_specs=pl.BlockSpec(memory_space=pl.ANY),
            scratch_shapes=[pltpu.VMEM((MOE_BLK, c), U32),
                            pltpu.SemaphoreType.DMA(()),
                            pltpu.SemaphoreType.DMA(())]),
        out_shape=jax.ShapeDtypeStruct((n_blocks * MOE_BLK, c), U32),
        compiler_params=pltpu.CompilerParams(dimension_semantics=("arbitrary",),
                                             vmem_limit_bytes=VMEM_LIMIT, has_side_effects=True),
        name="moe_dispatch",
    )(pend, padded, dest_flat, hp)


def _expert_kernel(blk_e_ref, n_used_ref, xs_ref, wgu_ref, bgu_ref, wd_ref, bd_ref, ys_ref):
    used = pl.program_id(0) < n_used_ref[0]

    @pl.when(used)
    def _():
        w = xs_ref[...]
        x = jnp.concatenate([_unpack_hi(w).astype(BF16), _unpack_lo(w).astype(BF16)], axis=-1)
        gu = jnp.dot(x, wgu_ref[0], preferred_element_type=F32) + bgu_ref[0]
        de = gu.shape[1] // 2
        g = jnp.minimum(gu[:, :de], SWIGLU_LIMIT)
        u = jnp.clip(gu[:, de:], -SWIGLU_LIMIT, SWIGLU_LIMIT)
        act = (u + 1.0) * (g * _sigmoid(SWIGLU_ALPHA * g))
        y = jnp.dot(act.astype(BF16), wd_ref[0], preferred_element_type=F32) + bd_ref[0]
        ys_ref[...] = _pack_bf16_pair(y)

    @pl.when(jnp.logical_not(used))
    def _():
        ys_ref[...] = jnp.zeros(ys_ref.shape, U32)


def _expert_call(blk_e, n_used, xs, wgu, bgu, wd, bd):
    p_rows, c = xs.shape
    n_exp, d, de2 = wgu.shape
    de = de2 // 2
    by_e3 = lambda i, be, nu: (be[i], 0, 0)
    return pl.pallas_call(
        _expert_kernel,
        grid_spec=pltpu.PrefetchScalarGridSpec(
            num_scalar_prefetch=2,
            grid=(p_rows // MOE_BLK,),
            in_specs=[pl.BlockSpec((MOE_BLK, c), lambda i, be, nu: (i, 0)),
                      pl.BlockSpec((1, d, de2), by_e3),
                      pl.BlockSpec((1, 1, de2), by_e3),
                      pl.BlockSpec((1, de, d), by_e3),
                      pl.BlockSpec((1, 1, d), by_e3)],
            out_specs=pl.BlockSpec((MOE_BLK, c), lambda i, be, nu: (i, 0))),
        out_shape=jax.ShapeDtypeStruct((p_rows, c), U32),
        compiler_params=_cparams(("arbitrary",)),
        name="moe_experts",
    )(blk_e, n_used, xs, wgu, bgu, wd, bd)


def _combine_kernel(dest_ref, meta_ref, x_ref, mods_ref, ys_ref, o_ref, buf, sem):
    def body(r, carry):
        for kk in range(TOP_K):
            d = dest_ref[r * TOP_K + kk]
            pltpu.make_async_copy(ys_ref.at[pl.ds(d, 1)], buf.at[kk, pl.ds(r, 1)], sem).start()
        return carry
    lax.fori_loop(0, TM, body, 0)
    for kk in range(TOP_K):
        pltpu.make_async_copy(ys_ref.at[pl.ds(0, TM)], buf.at[kk], sem).wait()

    meta = meta_ref[...]
    c = buf.shape[-1]
    y_hi = jnp.zeros((TM, c), F32)
    y_lo = jnp.zeros((TM, c), F32)
    for kk in range(TOP_K):
        gk = meta[:, TOP_K + kk:TOP_K + kk + 1]
        w = buf[kk]
        y_hi = y_hi + gk * _unpack_hi(w)
        y_lo = y_lo + gk * _unpack_lo(w)
    gf = mods_ref[0, 0][5:6, :]
    o_ref[0, :, :c] = x_ref[0, :, :c] + gf[:, :c] * y_hi
    o_ref[0, :, c:] = x_ref[0, :, c:] + gf[:, c:] * y_lo


def _combine_call(dest_flat, meta, x, mods, ys):
    b, l, d = x.shape
    t = l // TM
    c = ys.shape[-1]
    row = lambda bb, tt: (bb, tt, 0)
    return pl.pallas_call(
        _combine_kernel,
        grid=(b, t),
        in_specs=[pl.BlockSpec((TM * TOP_K,), lambda bb, tt: (bb * t + tt,), memory_space=pltpu.SMEM),
                  pl.BlockSpec((TM, LANES), lambda bb, tt: (bb * t + tt, 0)),
                  pl.BlockSpec((1, TM, d), row),
                  _mods_spec(d),
                  pl.BlockSpec(memory_space=pl.ANY)],
        out_specs=pl.BlockSpec((1, TM, d), row),
        out_shape=jax.ShapeDtypeStruct((b, l, d), F32),
        scratch_shapes=[pltpu.VMEM((TOP_K, TM, c), U32), pltpu.SemaphoreType.DMA(())],
        compiler_params=_cparams(("arbitrary", "arbitrary")),
        name="moe_combine",
    )(dest_flat, meta, x, mods, ys)


def _moe(x_new, hp, meta, counts, mods, wgu, bgu, wd, bd):
    n = hp.shape[0]
    n_exp = wgu.shape[0]
    n_blocks = -(-n * TOP_K // MOE_BLK) + n_exp
    counts = counts.reshape(n_exp).astype(jnp.int32)
    padded = (counts + MOE_BLK - 1) // MOE_BLK * MOE_BLK
    pend = jnp.cumsum(padded).astype(jnp.int32)
    pstart = pend - padded
    top_e = meta[:, 0:TOP_K].astype(jnp.int32)
    rank = meta[:, 2 * TOP_K:3 * TOP_K].astype(jnp.int32)
    dest = (pstart[top_e] + rank).reshape(n * TOP_K)
    blk_row0 = jnp.arange(n_blocks, dtype=jnp.int32) * MOE_BLK
    blk_e = jnp.minimum(jnp.sum((pend[None, :] <= blk_row0[:, None]).astype(jnp.int32), axis=1),
                        n_exp - 1)
    xs = _dispatch_call(pend, padded, dest, hp, n_blocks)
    ys = _expert_call(blk_e, pend[n_exp - 1:] // MOE_BLK, xs, wgu, bgu, wd, bd)
    return _combine_call(dest, meta, x_new, mods, ys)


def _final_norm_kernel(x_ref, g_ref, o_ref):
    x = x_ref[0]
    o_ref[0] = x * lax.rsqrt(jnp.mean(x * x, axis=-1, keepdims=True) + EPS) * g_ref[...]


def _final_norm_call(x, gain, n_ctx):
    b, l, d = x.shape
    s = l - n_ctx
    skip = n_ctx // TM
    return pl.pallas_call(
        _final_norm_kernel,
        grid=(b, s // TM),
        in_specs=[pl.BlockSpec((1, TM, d), lambda bb, tt: (bb, tt + skip, 0)),
                  pl.BlockSpec((1, d), lambda bb, tt: (0, 0))],
        out_specs=pl.BlockSpec((1, TM, d), lambda bb, tt: (bb, tt, 0)),
        out_shape=jax.ShapeDtypeStruct((b, s, d), F32),
        compiler_params=_cparams(("parallel", "parallel")),
        name="final_norm",
    )(x, gain.reshape(1, d))


def _rope_tables(s, n_ctx):
    t = jnp.arange(s, dtype=jnp.int32)
    inv_freq = ROPE_THETA ** (-jnp.arange(ROPE_AXIS_DIM // 2, dtype=F32) * 2.0 / ROPE_AXIS_DIM)
    ang_r = (t // GRID_W).astype(F32)[:, None] * inv_freq
    ang_c = (t % GRID_W).astype(F32)[:, None] * inv_freq
    cr, sr, cc, sc = jnp.cos(ang_r), jnp.sin(ang_r), jnp.cos(ang_c), jnp.sin(ang_c)
    cos_t = jnp.concatenate([cr, cr, cc, cc], axis=-1)
    sin_t = jnp.concatenate([-sr, sr, -sc, sc], axis=-1)
    cos_t = jnp.concatenate([jnp.ones((n_ctx, HEAD_DIM), F32), cos_t], axis=0)
    sin_t = jnp.concatenate([jnp.zeros((n_ctx, HEAD_DIM), F32), sin_t], axis=0)
    return cos_t, sin_t


def kernel(x, c, ctx, c_ctx, mod_w, mod_b, ap_w_in, ap_q_gain, ap_k_gain, ap_pool_w, ap_pool_scale, ap_w_out, rg_w_in, rg_conv_w, rg_conv_b, rg_gate_r_w, rg_gate_r_b, rg_gate_i_w, rg_gate_i_b, rg_lambda, rg_w_out, moe_router_w, moe_router_b, moe_w_gu, moe_b_gu, moe_w_down, moe_b_down, final_norm_g):
    b, s, d = x.shape
    n_ctx = ctx.shape[1]
    depth = mod_w.shape[0]
    assert n_ctx % TM == 0 and s % TM == 0 and s % (2 * ATT_CK) == 0 and s % GRID_W == 0
    assert b + 1 <= SUBLANES and d % (2 * LANES) == 0

    cv = jnp.zeros((SUBLANES, d), F32).at[:b].set(c).at[b].set(c_ctx)
    mods_all = _mods_call(cv, mod_w, mod_b).reshape(depth, SUBLANES, 6, d)
    cos_t, sin_t = _rope_tables(s, n_ctx)

    xx = jnp.concatenate([ctx, x], axis=1)
    d_rnn = rg_w_out.shape[1]
    rnn_blk = rg_gate_r_w.shape[-1]
    assert d_rnn % SCAN_LANES == 0 and d_rnn % (3 * LANES) == 0

    for l in range(depth):
        i = l // 2
        ml = mods_all[l]
        mods = jnp.stack([jnp.broadcast_to(ml[b][None], (b, 6, d)), ml[:b]])
        rw_hi = moe_router_w[l].astype(BF16)
        rw = jnp.stack([rw_hi, (moe_router_w[l] - rw_hi.astype(F32)).astype(BF16)])
        rb = moe_router_b[l].reshape(1, -1)
        if l % 2 == 0:
            q, k, v, p = _attn_in_call(xx, mods, ap_w_in[i].astype(BF16),
                                       ap_q_gain[i].reshape(1, -1), ap_k_gain[i].reshape(1, -1),
                                       cos_t, sin_t)
            ao = _attention_call(q, k, v, n_ctx)
            wo = ap_w_out[i].astype(BF16)
            x_new, hp, meta, counts = _attn_out_call(
                xx, mods, ao, p, ap_pool_w[i].astype(BF16), ap_pool_scale[i].reshape(1, -1),
                wo[:ATTN_DIM], wo[ATTN_DIM:], rw, rb, n_ctx)
        else:
            w_in = rg_w_in[i].astype(BF16)
            gg = _rg_in_call(xx, mods, w_in[:, :d_rnn], True, BF16, "rg_in_gate")
            u = _rg_in_call(xx, mods, w_in[:, d_rnn:], False, F32, "rg_in_u")
            hs = []
            for dr in range(2):
                slabs = _gate_slabs(rg_gate_r_w[i, dr], rg_gate_i_w[i, dr], d_rnn)
                gate_b = jnp.stack([rg_gate_r_b[i, dr], rg_gate_i_b[i, dr]])
                hs.append(_rg_scan_call(u, rg_conv_w[i], rg_conv_b[i].reshape(1, -1), slabs, gate_b,
                                        rg_lambda[i, dr], dr == 1, rnn_blk))
            x_new, hp, meta, counts = _rg_out_call(xx, mods, gg, hs[0], hs[1],
                                                   rg_w_out[i].astype(BF16), rw, rb)
        de = moe_w_down.shape[2]
        bgu = jnp.concatenate([moe_b_gu[l][:, 0::2], moe_b_gu[l][:, 1::2]], axis=-1)
        xx = _moe(x_new, hp, meta, counts, mods, _gu_split_call(moe_w_gu, l),
                  bgu.reshape(-1, 1, 2 * de),
                  moe_w_down[l].astype(BF16), moe_b_down[l].reshape(-1, 1, d))

    return _final_norm_call(xx, final_norm_g, n_ctx)
```

```python
import functools

import jax
import jax.numpy as jnp
import numpy as np
from jax import lax
from jax.experimental import pallas as pl
from jax.experimental.pallas import tpu as pltpu

F32 = jnp.float32
BF16 = jnp.bfloat16
U32 = jnp.uint32

EPS = 1e-6
GRID_W = 64
HEAD_DIM = 128
N_Q_HEADS = 12
N_KV_HEADS = 4
Q_PER_KV = N_Q_HEADS // N_KV_HEADS
ATTN_DIM = N_Q_HEADS * HEAD_DIM
KV_DIM = N_KV_HEADS * HEAD_DIM
ROPE_THETA = 10000.0
ROPE_AXIS_DIM = HEAD_DIM // 2
POOL_WINDOWS = (2, 4, 8, 16)
N_POOL_GROUPS = 4
N_RNN_BLOCKS = 16
CONV_WIDTH = 4
CONV_LEFT = 2
RG_C = 8.0
TOP_K = 4
SWIGLU_LIMIT = 7.0
SWIGLU_ALPHA = 1.702

LANES = 128
SUBLANES = 8
MXU_DIM = 256
VMEM_LIMIT = 56 * 1024 * 1024

TM = 256
HALO = SUBLANES
ATT_CK = 1024
LOG2E = 1.4426950408889634
MOE_BLK = 512
GATE_TN = 256
SCAN_LANES = 896


def _cparams(sem):
    return pltpu.CompilerParams(dimension_semantics=sem, vmem_limit_bytes=VMEM_LIMIT)


def _sigmoid(x):
    return 0.5 * jnp.tanh(0.5 * x) + 0.5


def _norm_mod(x, mods, i_shift, i_scale):
    ms = jnp.mean(x * x, axis=-1, keepdims=True)
    xn = x * lax.rsqrt(ms + EPS)
    return xn * (1.0 + mods[i_scale:i_scale + 1, :]) + mods[i_shift:i_shift + 1, :]


def _pack_bf16_pair(h):
    c = h.shape[-1] // 2
    bits = pltpu.bitcast(h.astype(BF16).astype(F32), U32)
    return (bits[:, :c] & jnp.uint32(0xFFFF0000)) | (bits[:, c:] >> jnp.uint32(16))


def _unpack_hi(w):
    return pltpu.bitcast(w & jnp.uint32(0xFFFF0000), F32)


def _unpack_lo(w):
    return pltpu.bitcast(w << jnp.uint32(16), F32)


def _mods_kernel(cv_ref, w_ref, b_ref, o_ref):
    cv = cv_ref[...]
    s = cv * _sigmoid(cv)
    o_ref[0] = jnp.dot(s.astype(BF16), w_ref[0].astype(BF16), preferred_element_type=F32) + b_ref[0]


def _mods_call(cv, mod_w, mod_b):
    depth, d, n6 = mod_w.shape
    tn = 1536
    return pl.pallas_call(
        _mods_kernel,
        grid=(depth, n6 // tn),
        in_specs=[pl.BlockSpec((SUBLANES, d), lambda l, j: (0, 0)),
                  pl.BlockSpec((1, d, tn), lambda l, j: (l, 0, j)),
                  pl.BlockSpec((1, 1, tn), lambda l, j: (l, 0, j))],
        out_specs=pl.BlockSpec((1, SUBLANES, tn), lambda l, j: (l, 0, j)),
        out_shape=jax.ShapeDtypeStruct((depth, SUBLANES, n6), F32),
        compiler_params=_cparams(("parallel", "parallel")),
        name="adaln_mods",
    )(cv, mod_w, mod_b.reshape(depth, 1, n6))


def _mods_spec(d):
    return pl.BlockSpec((1, 1, 6, d), lambda b, t: (jnp.minimum(t, 1), b, 0, 0))


def _attn_in_kernel(x_ref, mods_ref, w_ref, qg_ref, kg_ref, cos_ref, sin_ref,
                    q_ref, k_ref, v_ref, p_ref):
    h = _norm_mod(x_ref[0], mods_ref[0, 0], 0, 1).astype(BF16)
    z = jnp.dot(h, w_ref[...], preferred_element_type=F32)
    cos = cos_ref[...]
    sin = sin_ref[...]
    lane = lax.broadcasted_iota(jnp.int32, (TM, HEAD_DIM), 1)
    first = (lane % (ROPE_AXIS_DIM)) < (ROPE_AXIS_DIM // 2)

    def head(zh, gain, scale):
        n = zh * lax.rsqrt(jnp.mean(zh * zh, axis=-1, keepdims=True) + EPS) * gain
        half = ROPE_AXIS_DIM // 2
        swapped = jnp.where(first, pltpu.roll(n, HEAD_DIM - half, 1), pltpu.roll(n, half, 1))
        return (n * cos + swapped * sin) * scale

    qg = qg_ref[...]
    kg = kg_ref[...]
    for hh in range(N_Q_HEADS):
        sl = slice(hh * HEAD_DIM, (hh + 1) * HEAD_DIM)
        q_ref[0, :, sl] = head(z[:, sl], qg, LOG2E * HEAD_DIM ** -0.5).astype(BF16)
    for hh in range(N_KV_HEADS):
        zs = slice(ATTN_DIM + hh * HEAD_DIM, ATTN_DIM + (hh + 1) * HEAD_DIM)
        k_ref[0, hh] = head(z[:, zs], kg, 1.0).T.astype(BF16)
    v_ref[0] = z[:, ATTN_DIM + KV_DIM:ATTN_DIM + 2 * KV_DIM].astype(BF16)
    p_ref[0] = z[:, ATTN_DIM + 2 * KV_DIM:]


def _attn_in_call(x, mods, w_in, q_gain, k_gain, cos_t, sin_t):
    b, l, d = x.shape
    t = l // TM
    n_in = w_in.shape[1]
    pool_dim = n_in - ATTN_DIM - 2 * KV_DIM
    row = lambda bb, tt: (bb, tt, 0)
    return pl.pallas_call(
        _attn_in_kernel,
        grid=(b, t),
        in_specs=[pl.BlockSpec((1, TM, d), row),
                  _mods_spec(d),
                  pl.BlockSpec((d, n_in), lambda bb, tt: (0, 0)),
                  pl.BlockSpec((1, HEAD_DIM), lambda bb, tt: (0, 0)),
                  pl.BlockSpec((1, HEAD_DIM), lambda bb, tt: (0, 0)),
                  pl.BlockSpec((TM, HEAD_DIM), lambda bb, tt: (tt, 0)),
                  pl.BlockSpec((TM, HEAD_DIM), lambda bb, tt: (tt, 0))],
        out_specs=[pl.BlockSpec((1, TM, ATTN_DIM), row),
                   pl.BlockSpec((1, N_KV_HEADS, HEAD_DIM, TM), lambda bb, tt: (bb, 0, 0, tt)),
                   pl.BlockSpec((1, TM, KV_DIM), row),
                   pl.BlockSpec((1, TM, pool_dim), row)],
        out_shape=[jax.ShapeDtypeStruct((b, l, ATTN_DIM), BF16),
                   jax.ShapeDtypeStruct((b, N_KV_HEADS, HEAD_DIM, l), BF16),
                   jax.ShapeDtypeStruct((b, l, KV_DIM), BF16),
                   jax.ShapeDtypeStruct((b, l, pool_dim), F32)],
        compiler_params=_cparams(("parallel", "parallel")),
        name="attn_in",
    )(x, mods, w_in, q_gain, k_gain, cos_t, sin_t)


def _attention_kernel(q_ref, kt_ref, v_ref, o_ref, sa_sc, sb_sc, m_sc, acc_sc, *, n_ctx, n_lat_chunks):
    t = pl.program_id(2)
    q = jnp.concatenate([q_ref[0, :, g * HEAD_DIM:(g + 1) * HEAD_DIM] for g in range(Q_PER_KV)],
                        axis=0)
    m_sc[...] = jnp.full(m_sc.shape, -1e30, F32)
    acc_sc[...] = jnp.zeros(acc_sc.shape, F32)

    def scores(start, size):
        return jnp.dot(q, kt_ref[0, 0, :, pl.ds(start, size)], preferred_element_type=F32)

    def v_ext(start, size):
        return jnp.concatenate([v_ref[0, pl.ds(start, size), :], jnp.ones((size, HEAD_DIM), BF16)],
                               axis=-1)

    def softmax_pv(load_s, vx, width):
        for g in range(Q_PER_KV):
            rows = slice(g * TM, (g + 1) * TM)
            s = load_s(rows)
            blocks = [s[:, c * LANES:(c + 1) * LANES] for c in range(width // LANES)]
            mc = blocks[0]
            for blk in blocks[1:]:
                mc = jnp.maximum(mc, blk)
            m_prev = m_sc[rows, :]
            m_new = jnp.maximum(m_prev, jnp.broadcast_to(jnp.max(mc, axis=-1, keepdims=True),
                                                         (TM, LANES)))
            alpha = jnp.exp2(m_prev - m_new)
            p = jnp.concatenate([jnp.exp2(blk - m_new).astype(BF16) for blk in blocks], axis=-1)
            pv = jnp.dot(p, vx, preferred_element_type=F32)
            acc_sc[rows, :HEAD_DIM] = alpha * acc_sc[rows, :HEAD_DIM] + pv[:, :HEAD_DIM]
            acc_sc[rows, HEAD_DIM:] = alpha * acc_sc[rows, HEAD_DIM:] + pv[:, HEAD_DIM:]
            m_sc[rows, :] = m_new

    s0 = scores(0, n_ctx)
    softmax_pv(lambda rows: s0[rows], v_ext(0, n_ctx), n_ctx)

    @pl.when(t > 0)
    def _():
        def start(c):
            return pl.multiple_of(n_ctx + c * ATT_CK, LANES)

        def pair(c, prefetch):
            sb_sc[...] = scores(start(c + 1), ATT_CK)
            softmax_pv(lambda rows: sa_sc[rows, :], v_ext(start(c), ATT_CK), ATT_CK)
            if prefetch:
                sa_sc[...] = scores(start(c + 2), ATT_CK)
            softmax_pv(lambda rows: sb_sc[rows, :], v_ext(start(c + 1), ATT_CK), ATT_CK)

        sa_sc[...] = scores(n_ctx, ATT_CK)

        def body(i, carry):
            pair(2 * i, True)
            return carry
        lax.fori_loop(0, n_lat_chunks // 2 - 1, body, 0)
        pair(n_lat_chunks - 2, False)

    for g in range(Q_PER_KV):
        rows = slice(g * TM, (g + 1) * TM)
        o = acc_sc[rows, :HEAD_DIM] / acc_sc[rows, HEAD_DIM:]
        o_ref[0, :, g * HEAD_DIM:(g + 1) * HEAD_DIM] = o.astype(BF16)


def _attention_call(q, k, v, n_ctx):
    b, l, _ = q.shape
    t = l // TM
    gw = Q_PER_KV * HEAD_DIM
    kern = functools.partial(_attention_kernel, n_ctx=n_ctx, n_lat_chunks=(l - n_ctx) // ATT_CK)
    return pl.pallas_call(
        kern,
        grid=(b, N_KV_HEADS, t),
        in_specs=[pl.BlockSpec((1, TM, gw), lambda bb, hh, tt: (bb, tt, hh)),
                  pl.BlockSpec((1, 1, HEAD_DIM, l), lambda bb, hh, tt: (bb, hh, 0, 0)),
                  pl.BlockSpec((1, l, HEAD_DIM), lambda bb, hh, tt: (bb, 0, hh))],
        out_specs=pl.BlockSpec((1, TM, gw), lambda bb, hh, tt: (bb, tt, hh)),
        out_shape=jax.ShapeDtypeStruct((b, l, ATTN_DIM), BF16),
        scratch_shapes=[pltpu.VMEM((Q_PER_KV * TM, ATT_CK), F32),
                        pltpu.VMEM((Q_PER_KV * TM, ATT_CK), F32),
                        pltpu.VMEM((Q_PER_KV * TM, LANES), F32),
                        pltpu.VMEM((Q_PER_KV * TM, 2 * HEAD_DIM), F32)],
        compiler_params=_cparams(("parallel", "parallel", "parallel")),
        name="attention",
    )(q, k, v)


ROUTE_ROWS = TM


def _residual_and_route(x_ref, mix_rows, mods, rw_ref, rb_ref, cnt_sc, xo_ref, hp_ref, meta_ref):
    first_step = (pl.program_id(0) == 0) & (pl.program_id(1) == 0)

    @pl.when(first_step)
    def _():
        cnt_sc[...] = jnp.zeros(cnt_sc.shape, F32)

    n_exp = rw_ref.shape[-1]
    R = ROUTE_ROWS
    lane = lax.broadcasted_iota(jnp.int32, (R, n_exp), 1).astype(F32)
    lane_m = lax.broadcasted_iota(jnp.int32, (R, LANES), 1)
    ri = lax.broadcasted_iota(jnp.int32, (R, R), 0)
    ci = lax.broadcasted_iota(jnp.int32, (R, R), 1)
    tri = jnp.where(ci < ri, 1.0, 0.0).astype(BF16)
    running = cnt_sc[...]
    for r0 in range(0, TM, R):
        rows = slice(r0, r0 + R)
        xnew = x_ref[0, rows, :] + mods[2:3, :] * mix_rows(r0)
        xo_ref[0, rows, :] = xnew
        h2 = _norm_mod(xnew, mods, 3, 4)
        hp_ref[rows, :] = _pack_bf16_pair(h2)

        h_hi = h2.astype(BF16)
        h_lo = (h2 - h_hi.astype(F32)).astype(BF16)
        logits = (jnp.dot(h_hi, rw_ref[0], preferred_element_type=F32)
                  + jnp.dot(h_lo, rw_ref[0], preferred_element_type=F32)
                  + jnp.dot(h_hi, rw_ref[1], preferred_element_type=F32)) + rb_ref[...]
        onehots, tops, idxs = [], [], []
        cur = logits
        for _ in range(TOP_K):
            m = jnp.max(cur, axis=-1, keepdims=True)
            idx = jnp.min(jnp.where(cur == m, lane, float(n_exp)), axis=-1, keepdims=True)
            sel = lane == idx
            tops.append(m)
            idxs.append(idx)
            onehots.append(sel)
            cur = jnp.where(sel, -jnp.inf, cur)
        ws = [jnp.exp(m - tops[0]) for m in tops]
        denom = ws[0] + ws[1] + ws[2] + ws[3]

        counts = jnp.zeros((R, n_exp), F32)
        for sel in onehots:
            counts = counts + jnp.where(sel, 1.0, 0.0)
        before = jnp.dot(tri, counts.astype(BF16), preferred_element_type=F32) + running
        running = running + jnp.sum(counts, axis=0, keepdims=True)

        meta = jnp.zeros((R, LANES), F32)
        for kk in range(TOP_K):
            rank = jnp.sum(jnp.where(onehots[kk], before, 0.0), axis=-1, keepdims=True)
            meta = jnp.where(lane_m == kk, idxs[kk], meta)
            meta = jnp.where(lane_m == TOP_K + kk, ws[kk] / denom, meta)
            meta = jnp.where(lane_m == 2 * TOP_K + kk, rank, meta)
        meta_ref[rows, :] = meta
    cnt_sc[...] = running
    return running


def _route_out_specs(t, d, n_exp):
    flat = lambda bb, tt: (bb * t + tt, 0)
    return [pl.BlockSpec((1, TM, d), lambda bb, tt: (bb, tt, 0)),
            pl.BlockSpec((TM, d // 2), flat),
            pl.BlockSpec((TM, LANES), flat),
            pl.BlockSpec((1, n_exp), lambda bb, tt: (0, 0))]


def _route_out_shapes(b, l, d, n_exp):
    return [jax.ShapeDtypeStruct((b, l, d), F32),
            jax.ShapeDtypeStruct((b * l, d // 2), U32),
            jax.ShapeDtypeStruct((b * l, LANES), F32),
            jax.ShapeDtypeStruct((1, n_exp), F32)]


def _attn_out_kernel(x_ref, mods_ref, ao_ref, p_ref, pp_ref, pn_ref, pw_ref, ps_ref,
                     woa_ref, wop_ref, rw_ref, rb_ref,
                     xo_ref, hp_ref, meta_ref, cnt_ref, e_sc, cnt_sc, *, n_ctx, n_lat):
    t = pl.program_id(1)
    nt = pl.num_programs(1)
    prev_ok = t >= 2
    next_ok = (t >= 1) & (t <= nt - 2)
    e_sc[0:HALO, :] = jnp.where(prev_ok, pp_ref[0], 0.0)
    e_sc[HALO:HALO + TM, :] = p_ref[0]
    e_sc[HALO + TM:, :] = jnp.where(next_ok, pn_ref[0], 0.0)

    seq_len = jnp.where(t == 0, n_ctx, n_lat)
    gw = p_ref.shape[-1] // N_POOL_GROUPS
    R = ROUTE_ROWS

    def mix_rows(r0):
        rowi = r0 + lax.broadcasted_iota(jnp.int32, (R, 1), 0)
        pos = jnp.where(t == 0, rowi, (t - 1) * TM + rowi)
        mix = jnp.dot(ao_ref[0, r0:r0 + R, :], woa_ref[...], preferred_element_type=F32)
        for g, w in enumerate(POOL_WINDOWS):
            ls = slice(g * gw, (g + 1) * gw)
            base = r0 + HALO - w // 2
            acc = e_sc[base:base + R, ls]
            for j in range(1, w):
                acc = acc + e_sc[base + j:base + j + R, ls]
            lo = jnp.maximum(pos - w // 2, 0)
            hi = jnp.minimum(pos + w - w // 2, seq_len)
            dlt = acc / (hi - lo).astype(F32) - e_sc[r0 + HALO:r0 + HALO + R, ls]
            y = jnp.dot(dlt.astype(BF16), pw_ref[g], preferred_element_type=F32) * ps_ref[:, ls]
            mix = mix + jnp.dot(y.astype(BF16), wop_ref[ls, :], preferred_element_type=F32)
        return mix

    cnt_ref[...] = _residual_and_route(x_ref, mix_rows, mods_ref[0, 0], rw_ref, rb_ref, cnt_sc,
                                       xo_ref, hp_ref, meta_ref)


def _attn_out_call(x, mods, ao, p, pool_w, pool_scale, wo_a, wo_p, rw, rb, n_ctx):
    b, l, d = x.shape
    t = l // TM
    pool_dim = p.shape[-1]
    n_exp = rw.shape[-1]
    nh = l // HALO
    per = TM // HALO
    row = lambda bb, tt: (bb, tt, 0)
    const2 = lambda bb, tt: (0, 0)
    kern = functools.partial(_attn_out_kernel, n_ctx=n_ctx, n_lat=l - n_ctx)
    return pl.pallas_call(
        kern,
        grid=(b, t),
        in_specs=[pl.BlockSpec((1, TM, d), row),
                  _mods_spec(d),
                  pl.BlockSpec((1, TM, ATTN_DIM), row),
                  pl.BlockSpec((1, TM, pool_dim), row),
                  pl.BlockSpec((1, HALO, pool_dim), lambda bb, tt: (bb, jnp.maximum(tt * per - 1, 0), 0)),
                  pl.BlockSpec((1, HALO, pool_dim), lambda bb, tt: (bb, jnp.minimum((tt + 1) * per, nh - 1), 0)),
                  pl.BlockSpec(pool_w.shape, lambda bb, tt: (0, 0, 0)),
                  pl.BlockSpec((1, pool_dim), const2),
                  pl.BlockSpec(wo_a.shape, const2),
                  pl.BlockSpec(wo_p.shape, const2),
                  pl.BlockSpec(rw.shape, lambda bb, tt: (0, 0, 0)),
                  pl.BlockSpec((1, n_exp), const2)],
        out_specs=_route_out_specs(t, d, n_exp),
        out_shape=_route_out_shapes(b, l, d, n_exp),
        scratch_shapes=[pltpu.VMEM((TM + 2 * HALO, pool_dim), F32),
                        pltpu.VMEM((1, n_exp), F32)],
        compiler_params=_cparams(("arbitrary", "arbitrary")),
        name="attn_out",
    )(x, mods, ao, p, p, p, pool_w, pool_scale, wo_a, wo_p, rw, rb)


def _gelu_tanh(x):
    return 0.5 * x * (1.0 + jnp.tanh(0.7978845608028654 * (x + 0.044715 * (x * x * x))))


SEG_LEN = TM // SUBLANES


def _segment_perm(to_segment_major):
    ri = lax.broadcasted_iota(jnp.int32, (TM, TM), 0)
    ci = lax.broadcasted_iota(jnp.int32, (TM, TM), 1)
    if to_segment_major:
        hit = ci == (ri % SUBLANES) * SEG_LEN + ri // SUBLANES
    else:
        hit = ci == (ri % SEG_LEN) * SUBLANES + ri // SEG_LEN
    return jnp.where(hit, 1.0, 0.0).astype(BF16)


def _rg_in_kernel(x_ref, mods_ref, w_ref, o_ref, *, gelu):
    h = _norm_mod(x_ref[0], mods_ref[0, 0], 0, 1).astype(BF16)
    h = jnp.dot(_segment_perm(True), h, preferred_element_type=F32).astype(BF16)
    z = jnp.dot(h, w_ref[...], preferred_element_type=F32)
    if gelu:
        z = _gelu_tanh(z)
    o_ref[0] = z.astype(o_ref.dtype)


def _rg_in_call(x, mods, w, gelu, out_dtype, name):
    b, l, d = x.shape
    t = l // TM
    n = w.shape[1]
    row = lambda bb, tt: (bb, tt, 0)
    return pl.pallas_call(
        functools.partial(_rg_in_kernel, gelu=gelu),
        grid=(b, t),
        in_specs=[pl.BlockSpec((1, TM, d), row), _mods_spec(d),
                  pl.BlockSpec((d, n), lambda bb, tt: (0, 0))],
        out_specs=pl.BlockSpec((1, TM, n), row),
        out_shape=jax.ShapeDtypeStruct((b, l, n), out_dtype),
        compiler_params=_cparams(("parallel", "parallel")),
        name=name,
    )(x, mods, w)


def _gate_slab_starts(d_rnn, blk):
    n_slabs = -(-d_rnn // GATE_TN)
    spans = []
    for j in range(n_slabs):
        c0, c1 = j * GATE_TN, min((j + 1) * GATE_TN, d_rnn)
        lo = (c0 // blk) * blk
        hi = ((c1 - 1) // blk + 1) * blk
        spans.append((lo // LANES * LANES, -(-hi // LANES) * LANES))
    km = max(h - l for l, h in spans)
    starts = [min(l, d_rnn - km) for l, _ in spans]
    return starts, km


def _gate_slabs(r_w, i_w, d_rnn):
    blk = r_w.shape[-1]
    starts, km = _gate_slab_starts(d_rnn, blk)
    n_slabs = len(starts)
    pad = n_slabs * GATE_TN - d_rnn
    eye = jnp.eye(r_w.shape[0], dtype=r_w.dtype)
    dense = [jnp.pad((w[:, :, None, :] * eye[:, None, :, None]).reshape(d_rnn, d_rnn),
                     ((0, 0), (0, pad))) for w in (r_w, i_w)]
    slabs = [jnp.concatenate([dd[s:s + km, j * GATE_TN:(j + 1) * GATE_TN] for dd in dense], axis=1)
             for j, s in enumerate(starts)]
    return jnp.stack(slabs).astype(BF16)


def _rg_scan_kernel(u_ref, up_ref, un_ref, cw_ref, cb_ref, w_ref, gb_ref, lam_ref, h_ref,
                    e_sc, ucb_sc, uc_sc, a_sc, b_sc, carry_sc, *, reverse, starts, km):
    s = pl.program_id(1)
    nt = pl.num_programs(1)
    tau = jnp.where(s == 0, 0, nt - s) if reverse else s
    d_rnn = u_ref.shape[-1]

    @pl.when(s == 0)
    def _():
        carry_sc[...] = jnp.zeros(carry_sc.shape, F32)

    prev_ok = tau >= 2
    next_ok = (tau >= 1) & (tau <= nt - 2)
    G = SUBLANES
    sub = lax.broadcasted_iota(jnp.int32, (G, d_rnn), 0)
    for j in range(CONV_LEFT):
        src = u_ref[0, TM - (CONV_LEFT - j) * G:TM - (CONV_LEFT - j - 1) * G, :]
        edge = jnp.where(prev_ok, up_ref[0, (j + 1) * G - 1:(j + 1) * G, :], 0.0)
        e_sc[j * G:(j + 1) * G, :] = jnp.where(sub == 0, edge, pltpu.roll(src, 1, 0))
    e_sc[CONV_LEFT * G:CONV_LEFT * G + TM, :] = u_ref[0]
    edge = jnp.where(next_ok, un_ref[0, 0:1, :], 0.0)
    e_sc[CONV_LEFT * G + TM:, :] = jnp.where(sub == G - 1, edge, pltpu.roll(u_ref[0, 0:G, :], G - 1, 0))
    cw = cw_ref[...]
    n_conv_chunks = d_rnn // (3 * LANES)
    for c in range(n_conv_chunks):
        ls = slice(c * 3 * LANES, (c + 1) * 3 * LANES)
        uc = cb_ref[:, ls] + cw[0:1, ls] * e_sc[0:TM, ls]
        for j in range(1, CONV_WIDTH):
            uc = uc + cw[j:j + 1, ls] * e_sc[j * G:j * G + TM, ls]
        uc_sc[:, ls] = uc
        ucb_sc[:, ls] = uc.astype(BF16)

    lam = lam_ref[0]
    neg = -lam
    softplus = jnp.maximum(neg, 0.0) + jnp.log(1.0 + jnp.exp(-jnp.abs(neg)))
    gb = gb_ref[0]
    for j, st in enumerate(starts):
        n = min(GATE_TN, d_rnn - j * GATE_TN)
        ls = slice(j * GATE_TN, j * GATE_TN + n)
        z = jnp.dot(ucb_sc[:, st:st + km], w_ref[0, j], preferred_element_type=F32)
        r = _sigmoid(z[:, :n] + gb[0:1, ls])
        i = _sigmoid(z[:, GATE_TN:GATE_TN + n] + gb[1:2, ls])
        log_a = (-RG_C) * r * softplus[:, ls]
        a = jnp.exp(log_a)
        mult = jnp.sqrt(jnp.maximum(1.0 - a * a, 0.0))
        b_sc[:, ls] = mult * (i * uc_sc[:, ls])
        a_sc[:, ls] = a

    steps = range(SEG_LEN - 1, -1, -1) if reverse else range(SEG_LEN)
    rowi = lax.broadcasted_iota(jnp.int32, (SUBLANES, SCAN_LANES), 0)
    grp = lambda t: slice(t * SUBLANES, (t + 1) * SUBLANES)
    for c in range(d_rnn // SCAN_LANES):
        ls = slice(c * SCAN_LANES, (c + 1) * SCAN_LANES)
        tot_h = jnp.zeros((SUBLANES, SCAN_LANES), F32)
        tot_p = jnp.ones((SUBLANES, SCAN_LANES), F32)
        for t in steps:
            a = a_sc[grp(t), ls]
            tot_h = a * tot_h + b_sc[grp(t), ls]
            tot_p = a * tot_p

        for sft in (1, 2, 4):
            if reverse:
                ok = rowi < SUBLANES - sft
                shift = SUBLANES - sft
            else:
                ok = rowi >= sft
                shift = sft
            pp = jnp.where(ok, pltpu.roll(tot_p, shift, 0), 1.0)
            hp = jnp.where(ok, pltpu.roll(tot_h, shift, 0), 0.0)
            tot_h = tot_p * hp + tot_h
            tot_p = tot_p * pp
        carry = carry_sc[:, ls]
        after = tot_p * carry + tot_h
        if reverse:
            h = jnp.where(rowi < SUBLANES - 1, pltpu.roll(after, SUBLANES - 1, 0), carry)
            carry_sc[:, ls] = jnp.broadcast_to(after[0:1, :], (SUBLANES, SCAN_LANES))
        else:
            h = jnp.where(rowi >= 1, pltpu.roll(after, 1, 0), carry)
            carry_sc[:, ls] = jnp.broadcast_to(after[SUBLANES - 1:SUBLANES, :], (SUBLANES, SCAN_LANES))

        for t in steps:
            h = a_sc[grp(t), ls] * h + b_sc[grp(t), ls]
            b_sc[grp(t), ls] = h

    h_ref[0] = b_sc[...].astype(h_ref.dtype)


def _rg_scan_call(u, conv_w, conv_b, slabs, gate_b, lam, reverse, blk):
    b, l, d_rnn = u.shape
    t = l // TM
    starts, km = _gate_slab_starts(d_rnn, blk)
    lead = CONV_LEFT * SUBLANES
    per_lead = TM // lead
    per_tail = TM // SUBLANES

    def tile(ss):
        return jnp.where(ss == 0, 0, t - ss) if reverse else ss

    kern = functools.partial(_rg_scan_kernel, reverse=reverse, starts=tuple(starts), km=km)
    const2 = lambda bb, ss: (0, 0)
    return pl.pallas_call(
        kern,
        grid=(b, t),
        in_specs=[pl.BlockSpec((1, TM, d_rnn), lambda bb, ss: (bb, tile(ss), 0)),
                  pl.BlockSpec((1, lead, d_rnn),
                               lambda bb, ss: (bb, jnp.maximum(tile(ss) * per_lead - 1, 0), 0)),
                  pl.BlockSpec((1, SUBLANES, d_rnn),
                               lambda bb, ss: (bb, jnp.minimum((tile(ss) + 1) * per_tail,
                                                               l // SUBLANES - 1), 0)),
                  pl.BlockSpec(conv_w.shape, const2),
                  pl.BlockSpec((1, d_rnn), const2),
                  pl.BlockSpec((1,) + slabs.shape, lambda bb, ss: (0, 0, 0, 0)),
                  pl.BlockSpec((1, 2, d_rnn), lambda bb, ss: (0, 0, 0)),
                  pl.BlockSpec((1, 1, d_rnn), lambda bb, ss: (0, 0, 0))],
        out_specs=pl.BlockSpec((1, TM, d_rnn), lambda bb, ss: (bb, tile(ss), 0)),
        out_shape=jax.ShapeDtypeStruct((b, l, d_rnn), BF16),
        scratch_shapes=[pltpu.VMEM((TM + (CONV_WIDTH - 1) * SUBLANES, d_rnn), F32),
                        pltpu.VMEM((TM, d_rnn), BF16),
                        pltpu.VMEM((TM, d_rnn), F32),
                        pltpu.VMEM((TM, d_rnn), F32),
                        pltpu.VMEM((TM, d_rnn), F32),
                        pltpu.VMEM((SUBLANES, d_rnn), F32)],
        compiler_params=_cparams(("arbitrary", "arbitrary")),
        name="rg_scan_bwd" if reverse else "rg_scan_fwd",
    )(u, u, u, conv_w, conv_b, slabs[None], gate_b[None], lam[None, None])


def _rg_out_kernel(x_ref, mods_ref, g_ref, hf_ref, hb_ref, wo_ref, rw_ref, rb_ref,
                   xo_ref, hp_ref, meta_ref, cnt_ref, cnt_sc):
    gated = g_ref[0].astype(F32) * (hf_ref[0].astype(F32) + hb_ref[0].astype(F32))
    perm = _segment_perm(False)
    gated = gated.astype(BF16)

    def mix_rows(r0):
        rows = jnp.dot(perm[r0:r0 + ROUTE_ROWS, :], gated, preferred_element_type=F32)
        return jnp.dot(rows.astype(BF16), wo_ref[...], preferred_element_type=F32)

    cnt_ref[...] = _residual_and_route(x_ref, mix_rows, mods_ref[0, 0], rw_ref, rb_ref, cnt_sc,
                                       xo_ref, hp_ref, meta_ref)


def _rg_out_call(x, mods, gg, hf, hb, w_out, rw, rb):
    b, l, d = x.shape
    t = l // TM
    d_rnn = gg.shape[-1]
    n_exp = rw.shape[-1]
    row = lambda bb, tt: (bb, tt, 0)
    const2 = lambda bb, tt: (0, 0)
    return pl.pallas_call(
        _rg_out_kernel,
        grid=(b, t),
        in_specs=[pl.BlockSpec((1, TM, d), row), _mods_spec(d),
                  pl.BlockSpec((1, TM, d_rnn), row),
                  pl.BlockSpec((1, TM, d_rnn), row),
                  pl.BlockSpec((1, TM, d_rnn), row),
                  pl.BlockSpec(w_out.shape, const2),
                  pl.BlockSpec(rw.shape, lambda bb, tt: (0, 0, 0)),
                  pl.BlockSpec((1, n_exp), const2)],
        out_specs=_route_out_specs(t, d, n_exp),
        out_shape=_route_out_shapes(b, l, d, n_exp),
        scratch_shapes=[pltpu.VMEM((1, n_exp), F32)],
        compiler_params=_cparams(("arbitrary", "arbitrary")),
        name="rg_out",
    )(x, mods, gg, hf, hb, w_out, rw, rb)


GU_ROWS = 512
EXPERT_ROWS = 256


def _gu_split_kernel(w_ref, o_ref):
    w = w_ref[0, 0].astype(BF16)
    de = w.shape[1] // 2
    pair = 2 * LANES
    ri = lax.broadcasted_iota(jnp.int32, (pair, pair), 0)
    ci = lax.broadcasted_iota(jnp.int32, (pair, pair), 1)
    src = jnp.where(ci < LANES, 2 * ci, 2 * (ci - LANES) + 1)
    sel = jnp.where(ri == src, 1.0, 0.0).astype(BF16)
    for j in range(w.shape[1] // pair):
        r = jnp.dot(w[:, j * pair:(j + 1) * pair], sel, preferred_element_type=F32)
        o_ref[0, :, j * LANES:(j + 1) * LANES] = r[:, :LANES].astype(BF16)
        o_ref[0, :, de + j * LANES:de + (j + 1) * LANES] = r[:, LANES:].astype(BF16)


def _gu_split_call(w_gu_all, layer):
    _, n_exp, d, de2 = w_gu_all.shape
    return pl.pallas_call(
        _gu_split_kernel,
        grid=(n_exp, d // GU_ROWS),
        in_specs=[pl.BlockSpec((1, 1, GU_ROWS, de2), lambda e, r: (layer, e, r, 0))],
        out_specs=pl.BlockSpec((1, GU_ROWS, de2), lambda e, r: (e, r, 0)),
        out_shape=jax.ShapeDtypeStruct((n_exp, d, de2), BF16),
        compiler_params=_cparams(("parallel", "parallel")),
        name="moe_gu_split",
    )(w_gu_all)


def _dispatch_kernel(pend_ref, padded_ref, dest_ref, hp_ref, xs_ref, zero_sc, sem, zsem, *, n_blocks):
    step = pl.program_id(0)
    n_exp = pend_ref.shape[0]

    @pl.when(step == 0)
    def _():
        zero_sc[...] = jnp.zeros(zero_sc.shape, U32)

        def zero_block(row0):
            rows = pl.ds(pl.multiple_of(row0, MOE_BLK), MOE_BLK)
            return pltpu.make_async_copy(zero_sc, xs_ref.at[rows], zsem)

        first_tail = pend_ref[n_exp - 1] // MOE_BLK
        for go in (lambda cp: cp.start(), lambda cp: cp.wait()):
            def per_expert(e, carry, go=go):
                @pl.when(padded_ref[e] > 0)
                def _():
                    go(zero_block(pend_ref[e] - MOE_BLK))
                return carry
            lax.fori_loop(0, n_exp, per_expert, 0)

            def per_tail(i, carry, go=go):
                go(zero_block(i * MOE_BLK))
                return carry
            lax.fori_loop(first_tail, n_blocks, per_tail, 0)

    for r in range(TM):
        for kk in range(TOP_K):
            d = dest_ref[r * TOP_K + kk]
            pltpu.make_async_copy(hp_ref.at[pl.ds(r, 1)], xs_ref.at[pl.ds(d, 1)], sem).start()
    for _ in range(TOP_K):
        pltpu.make_async_copy(hp_ref, xs_ref.at[pl.ds(0, TM)], sem).wait()


def _dispatch_call(pend, padded, dest_flat, hp, n_blocks):
    n, c = hp.shape
    kern = functools.partial(_dispatch_kernel, n_blocks=n_blocks)
    return pl.pallas_call(
        kern,
        grid_spec=pltpu.PrefetchScalarGridSpec(
            num_scalar_prefetch=2,
            grid=(n // TM,),
            in_specs=[pl.BlockSpec((TM * TOP_K,), lambda i, *_: (i,), memory_space=pltpu.SMEM),
                      pl.BlockSpec((TM, c), lambda i, *_: (i, 0))],
            out_specs=pl.BlockSpec(memory_space=pl.ANY),
            scratch_shapes=[pltpu.VMEM((MOE_BLK, c), U32),
                            pltpu.SemaphoreType.DMA(()),
                            pltpu.SemaphoreType.DMA(())]),
        out_shape=jax.ShapeDtypeStruct((n_blocks * MOE_BLK, c), U32),
        compiler_params=pltpu.CompilerParams(dimension_semantics=("arbitrary",),
                                             vmem_limit_bytes=VMEM_LIMIT, has_side_effects=True),
        name="moe_dispatch",
    )(pend, padded, dest_flat, hp)


def _expert_kernel(blk_e_ref, n_used_ref, xs_ref, wgu_ref, bgu_ref, wd_ref, bd_ref, ys_ref):
    used = pl.program_id(0) < n_used_ref[0]

    @pl.when(used)
    def _():
        for r0 in range(0, MOE_BLK, EXPERT_ROWS):
            rows = slice(r0, r0 + EXPERT_ROWS)
            w = xs_ref[rows, :]
            x = jnp.concatenate([_unpack_hi(w).astype(BF16), _unpack_lo(w).astype(BF16)], axis=-1)
            gu = jnp.dot(x, wgu_ref[0], preferred_element_type=F32) + bgu_ref[0]
            de = gu.shape[1] // 2
            g = jnp.minimum(gu[:, :de], SWIGLU_LIMIT)
            u = jnp.clip(gu[:, de:], -SWIGLU_LIMIT, SWIGLU_LIMIT)
            act = (u + 1.0) * (g * _sigmoid(SWIGLU_ALPHA * g))
            y = jnp.dot(act.astype(BF16), wd_ref[0], preferred_element_type=F32) + bd_ref[0]
            ys_ref[rows, :] = _pack_bf16_pair(y)

    @pl.when(jnp.logical_not(used))
    def _():
        ys_ref[...] = jnp.zeros(ys_ref.shape, U32)


def _expert_call(blk_e, n_used, xs, wgu, bgu, wd, bd):
    p_rows, c = xs.shape
    n_exp, d, de2 = wgu.shape
    de = de2 // 2
    by_e3 = lambda i, be, nu: (be[i], 0, 0)
    return pl.pallas_call(
        _expert_kernel,
        grid_spec=pltpu.PrefetchScalarGridSpec(
            num_scalar_prefetch=2,
            grid=(p_rows // MOE_BLK,),
            in_specs=[pl.BlockSpec((MOE_BLK, c), lambda i, be, nu: (i, 0)),
                      pl.BlockSpec((1, d, de2), by_e3),
                      pl.BlockSpec((1, 1, de2), by_e3),
                      pl.BlockSpec((1, de, d), by_e3),
                      pl.BlockSpec((1, 1, d), by_e3)],
            out_specs=pl.BlockSpec((MOE_BLK, c), lambda i, be, nu: (i, 0))),
        out_shape=jax.ShapeDtypeStruct((p_rows, c), U32),
        compiler_params=_cparams(("arbitrary",)),
        name="moe_experts",
    )(blk_e, n_used, xs, wgu, bgu, wd, bd)


def _combine_kernel(dest_ref, dest_next_ref, meta_ref, x_ref, mods_ref, ys_ref, o_ref, buf, sems):
    step = pl.program_id(0) * pl.num_programs(1) + pl.program_id(1)
    n_steps = pl.num_programs(0) * pl.num_programs(1)
    slot = step % 2

    def request(idx_ref, s):
        for r in range(TM):
            for kk in range(TOP_K):
                d = idx_ref[r * TOP_K + kk]
                pltpu.make_async_copy(ys_ref.at[pl.ds(d, 1)], buf.at[s, kk, pl.ds(r, 1)],
                                      sems.at[s]).start()

    def wait(s):
        for kk in range(TOP_K):
            pltpu.make_async_copy(ys_ref.at[pl.ds(0, TM)], buf.at[s, kk], sems.at[s]).wait()

    @pl.when(step == 0)
    def _():
        request(dest_ref, 0)

    request(dest_next_ref, 1 - slot)
    wait(slot)

    meta = meta_ref[...]
    c = buf.shape[-1]
    y_hi = jnp.zeros((TM, c), F32)
    y_lo = jnp.zeros((TM, c), F32)
    for kk in range(TOP_K):
        gk = meta[:, TOP_K + kk:TOP_K + kk + 1]
        w = buf[slot, kk]
        y_hi = y_hi + gk * _unpack_hi(w)
        y_lo = y_lo + gk * _unpack_lo(w)
    gf = mods_ref[0, 0][5:6, :]
    o_ref[0, :, :c] = x_ref[0, :, :c] + gf[:, :c] * y_hi
    o_ref[0, :, c:] = x_ref[0, :, c:] + gf[:, c:] * y_lo

    @pl.when(step == n_steps - 1)
    def _():
        wait(1 - slot)


def _combine_call(dest_flat, meta, x, mods, ys):
    b, l, d = x.shape
    t = l // TM
    c = ys.shape[-1]
    row = lambda bb, tt: (bb, tt, 0)
    return pl.pallas_call(
        _combine_kernel,
        grid=(b, t),
        in_specs=[pl.BlockSpec((TM * TOP_K,), lambda bb, tt: (bb * t + tt,), memory_space=pltpu.SMEM),
                  pl.BlockSpec((TM * TOP_K,), lambda bb, tt: (jnp.minimum(bb * t + tt + 1, b * t - 1),),
                               memory_space=pltpu.SMEM),
                  pl.BlockSpec((TM, LANES), lambda bb, tt: (bb * t + tt, 0)),
                  pl.BlockSpec((1, TM, d), row),
                  _mods_spec(d),
                  pl.BlockSpec(memory_space=pl.ANY)],
        out_specs=pl.BlockSpec((1, TM, d), row),
        out_shape=jax.ShapeDtypeStruct((b, l, d), F32),
        scratch_shapes=[pltpu.VMEM((2, TOP_K, TM, c), U32), pltpu.SemaphoreType.DMA((2,))],
        compiler_params=_cparams(("arbitrary", "arbitrary")),
        name="moe_combine",
    )(dest_flat, dest_flat, meta, x, mods, ys)


def _moe(x_new, hp, meta, counts, mods, wgu, bgu, wd, bd):
    n = hp.shape[0]
    n_exp = wgu.shape[0]
    n_blocks = -(-n * TOP_K // MOE_BLK) + n_exp
    counts = counts.reshape(n_exp).astype(jnp.int32)
    padded = (counts + MOE_BLK - 1) // MOE_BLK * MOE_BLK
    pend = jnp.cumsum(padded).astype(jnp.int32)
    pstart = pend - padded
    top_e = meta[:, 0:TOP_K].astype(jnp.int32)
    rank = meta[:, 2 * TOP_K:3 * TOP_K].astype(jnp.int32)
    dest = (pstart[top_e] + rank).reshape(n * TOP_K)
    blk_row0 = jnp.arange(n_blocks, dtype=jnp.int32) * MOE_BLK
    blk_e = jnp.minimum(jnp.sum((pend[None, :] <= blk_row0[:, None]).astype(jnp.int32), axis=1),
                        n_exp - 1)
    xs = _dispatch_call(pend, padded, dest, hp, n_blocks)
    ys = _expert_call(blk_e, pend[n_exp - 1:] // MOE_BLK, xs, wgu, bgu, wd, bd)
    return _combine_call(dest, meta, x_new, mods, ys)


def _final_norm_kernel(x_ref, g_ref, o_ref):
    x = x_ref[0]
    o_ref[0] = x * lax.rsqrt(jnp.mean(x * x, axis=-1, keepdims=True) + EPS) * g_ref[...]


def _final_norm_call(x, gain, n_ctx):
    b, l, d = x.shape
    s = l - n_ctx
    skip = n_ctx // TM
    return pl.pallas_call(
        _final_norm_kernel,
        grid=(b, s // TM),
        in_specs=[pl.BlockSpec((1, TM, d), lambda bb, tt: (bb, tt + skip, 0)),
                  pl.BlockSpec((1, d), lambda bb, tt: (0, 0))],
        out_specs=pl.BlockSpec((1, TM, d), lambda bb, tt: (bb, tt, 0)),
        out_shape=jax.ShapeDtypeStruct((b, s, d), F32),
        compiler_params=_cparams(("parallel", "parallel")),
        name="final_norm",
    )(x, gain.reshape(1, d))


def _rope_tables(s, n_ctx):
    t = jnp.arange(s, dtype=jnp.int32)
    inv_freq = ROPE_THETA ** (-jnp.arange(ROPE_AXIS_DIM // 2, dtype=F32) * 2.0 / ROPE_AXIS_DIM)
    ang_r = (t // GRID_W).astype(F32)[:, None] * inv_freq
    ang_c = (t % GRID_W).astype(F32)[:, None] * inv_freq
    cr, sr, cc, sc = jnp.cos(ang_r), jnp.sin(ang_r), jnp.cos(ang_c), jnp.sin(ang_c)
    cos_t = jnp.concatenate([cr, cr, cc, cc], axis=-1)
    sin_t = jnp.concatenate([-sr, sr, -sc, sc], axis=-1)
    cos_t = jnp.concatenate([jnp.ones((n_ctx, HEAD_DIM), F32), cos_t], axis=0)
    sin_t = jnp.concatenate([jnp.zeros((n_ctx, HEAD_DIM), F32), sin_t], axis=0)
    return cos_t, sin_t


def kernel(x, c, ctx, c_ctx, mod_w, mod_b, ap_w_in, ap_q_gain, ap_k_gain, ap_pool_w, ap_pool_scale, ap_w_out, rg_w_in, rg_conv_w, rg_conv_b, rg_gate_r_w, rg_gate_r_b, rg_gate_i_w, rg_gate_i_b, rg_lambda, rg_w_out, moe_router_w, moe_router_b, moe_w_gu, moe_b_gu, moe_w_down, moe_b_down, final_norm_g):
    b, s, d = x.shape
    n_ctx = ctx.shape[1]
    depth = mod_w.shape[0]
    assert n_ctx % TM == 0 and s % TM == 0 and s % (2 * ATT_CK) == 0 and s % GRID_W == 0
    assert b + 1 <= SUBLANES and d % (2 * LANES) == 0

    cv = jnp.zeros((SUBLANES, d), F32).at[:b].set(c).at[b].set(c_ctx)
    mods_all = _mods_call(cv, mod_w, mod_b).reshape(depth, SUBLANES, 6, d)
    cos_t, sin_t = _rope_tables(s, n_ctx)

    xx = jnp.concatenate([ctx, x], axis=1)
    d_rnn = rg_w_out.shape[1]
    rnn_blk = rg_gate_r_w.shape[-1]
    assert d_rnn % SCAN_LANES == 0 and d_rnn % (3 * LANES) == 0

    for l in range(depth):
        i = l // 2
        ml = mods_all[l]
        mods = jnp.stack([jnp.broadcast_to(ml[b][None], (b, 6, d)), ml[:b]])
        rw_hi = moe_router_w[l].astype(BF16)
        rw = jnp.stack([rw_hi, (moe_router_w[l] - rw_hi.astype(F32)).astype(BF16)])
        rb = moe_router_b[l].reshape(1, -1)
        if l % 2 == 0:
            q, k, v, p = _attn_in_call(xx, mods, ap_w_in[i].astype(BF16),
                                       ap_q_gain[i].reshape(1, -1), ap_k_gain[i].reshape(1, -1),
                                       cos_t, sin_t)
            ao = _attention_call(q, k, v, n_ctx)
            wo = ap_w_out[i].astype(BF16)
            x_new, hp, meta, counts = _attn_out_call(
                xx, mods, ao, p, ap_pool_w[i].astype(BF16), ap_pool_scale[i].reshape(1, -1),
                wo[:ATTN_DIM], wo[ATTN_DIM:], rw, rb, n_ctx)
        else:
            w_in = rg_w_in[i].astype(BF16)
            gg = _rg_in_call(xx, mods, w_in[:, :d_rnn], True, BF16, "rg_in_gate")
            u = _rg_in_call(xx, mods, w_in[:, d_rnn:], False, F32, "rg_in_u")
            hs = []
            for dr in range(2):
                slabs = _gate_slabs(rg_gate_r_w[i, dr], rg_gate_i_w[i, dr], d_rnn)
                gate_b = jnp.stack([rg_gate_r_b[i, dr], rg_gate_i_b[i, dr]])
                hs.append(_rg_scan_call(u, rg_conv_w[i], rg_conv_b[i].reshape(1, -1), slabs, gate_b,
                                        rg_lambda[i, dr], dr == 1, rnn_blk))
            x_new, hp, meta, counts = _rg_out_call(xx, mods, gg, hs[0], hs[1],
                                                   rg_w_out[i].astype(BF16), rw, rb)
        de = moe_w_down.shape[2]
        bgu = jnp.concatenate([moe_b_gu[l][:, 0::2], moe_b_gu[l][:, 1::2]], axis=-1)
        xx = _moe(x_new, hp, meta, counts, mods, _gu_split_call(moe_w_gu, l),
                  bgu.reshape(-1, 1, 2 * de),
                  moe_w_down[l].astype(BF16), moe_b_down[l].reshape(-1, 1, d))

    return _final_norm_call(xx, final_norm_g, n_ctx)
```

```python
import functools
from typing import NamedTuple

import jax
import jax.numpy as jnp
import numpy as np
from jax import lax
from jax.experimental import pallas as pl
from jax.experimental.pallas import tpu as pltpu

F32 = jnp.float32
BF16 = jnp.bfloat16
U32 = jnp.uint32

EPS = 1e-6
GRID_W = 64
HEAD_DIM = 128
N_Q_HEADS = 12
N_KV_HEADS = 4
Q_PER_KV = N_Q_HEADS // N_KV_HEADS
ATTN_DIM = N_Q_HEADS * HEAD_DIM
KV_DIM = N_KV_HEADS * HEAD_DIM
ROPE_THETA = 10000.0
ROPE_AXIS_DIM = HEAD_DIM // 2
POOL_WINDOWS = (2, 4, 8, 16)
N_POOL_GROUPS = 4
N_RNN_BLOCKS = 16
CONV_WIDTH = 4
CONV_LEFT = 2
RG_C = 8.0
TOP_K = 4
SWIGLU_LIMIT = 7.0
SWIGLU_ALPHA = 1.702

LANES = 128
SUBLANES = 8
MXU_DIM = 256
VMEM_LIMIT = 56 * 1024 * 1024

TM = 256
HALO = SUBLANES
ATT_CK = 1024
LOG2E = 1.4426950408889634
MOE_BLK = 512
GATE_TN = 256
SCAN_LANES = 896


def _cparams(sem):
    return pltpu.CompilerParams(dimension_semantics=sem, vmem_limit_bytes=VMEM_LIMIT)


def _sigmoid(x):
    return 0.5 * jnp.tanh(0.5 * x) + 0.5


def _norm_mod(x, mods, i_shift, i_scale):
    ms = jnp.mean(x * x, axis=-1, keepdims=True)
    xn = x * lax.rsqrt(ms + EPS)
    return xn * (1.0 + mods[i_scale:i_scale + 1, :]) + mods[i_shift:i_shift + 1, :]


def _pack_bf16_pair(h):
    c = h.shape[-1] // 2
    bits = pltpu.bitcast(h.astype(BF16).astype(F32), U32)
    return (bits[:, :c] & jnp.uint32(0xFFFF0000)) | (bits[:, c:] >> jnp.uint32(16))


def _unpack_hi(w):
    return pltpu.bitcast(w & jnp.uint32(0xFFFF0000), F32)


def _unpack_lo(w):
    return pltpu.bitcast(w << jnp.uint32(16), F32)


def _mods_kernel(cv_ref, w_ref, b_ref, o_ref):
    cv = cv_ref[...]
    s = cv * _sigmoid(cv)
    o_ref[0] = jnp.dot(s.astype(BF16), w_ref[0].astype(BF16), preferred_element_type=F32) + b_ref[0]


def _mods_call(cv, mod_w, mod_b):
    depth, d, n6 = mod_w.shape
    tn = 1536
    return pl.pallas_call(
        _mods_kernel,
        grid=(depth, n6 // tn),
        in_specs=[pl.BlockSpec((SUBLANES, d), lambda l, j: (0, 0)),
                  pl.BlockSpec((1, d, tn), lambda l, j: (l, 0, j)),
                  pl.BlockSpec((1, 1, tn), lambda l, j: (l, 0, j))],
        out_specs=pl.BlockSpec((1, SUBLANES, tn), lambda l, j: (l, 0, j)),
        out_shape=jax.ShapeDtypeStruct((depth, SUBLANES, n6), F32),
        compiler_params=_cparams(("parallel", "parallel")),
        name="adaln_mods",
    )(cv, mod_w, mod_b.reshape(depth, 1, n6))


def _mods_spec(d):
    return pl.BlockSpec((1, 1, 6, d), lambda b, t: (jnp.minimum(t, 1), b, 0, 0))


def _attn_in_kernel(x_ref, mods_ref, w_ref, qg_ref, kg_ref, cos_ref, sin_ref,
                    q_ref, k_ref, v_ref, p_ref):
    h = _norm_mod(x_ref[0], mods_ref[0, 0], 0, 1).astype(BF16)
    z = jnp.dot(h, w_ref[...], preferred_element_type=F32)
    cos = cos_ref[...]
    sin = sin_ref[...]
    lane = lax.broadcasted_iota(jnp.int32, (TM, HEAD_DIM), 1)
    first = (lane % (ROPE_AXIS_DIM)) < (ROPE_AXIS_DIM // 2)

    def head(zh, gain, scale):
        n = zh * lax.rsqrt(jnp.mean(zh * zh, axis=-1, keepdims=True) + EPS) * gain
        half = ROPE_AXIS_DIM // 2
        swapped = jnp.where(first, pltpu.roll(n, HEAD_DIM - half, 1), pltpu.roll(n, half, 1))
        return (n * cos + swapped * sin) * scale

    qg = qg_ref[...]
    kg = kg_ref[...]
    for hh in range(N_Q_HEADS):
        sl = slice(hh * HEAD_DIM, (hh + 1) * HEAD_DIM)
        q_ref[0, :, sl] = head(z[:, sl], qg, LOG2E * HEAD_DIM ** -0.5).astype(BF16)
    for hh in range(N_KV_HEADS):
        zs = slice(ATTN_DIM + hh * HEAD_DIM, ATTN_DIM + (hh + 1) * HEAD_DIM)
        k_ref[0, hh] = head(z[:, zs], kg, 1.0).T.astype(BF16)
    v_ref[0] = z[:, ATTN_DIM + KV_DIM:ATTN_DIM + 2 * KV_DIM].astype(BF16)
    p_ref[0] = z[:, ATTN_DIM + 2 * KV_DIM:]


def _attn_in_call(x, mods, w_in, q_gain, k_gain, cos_t, sin_t):
    b, l, d = x.shape
    t = l // TM
    n_in = w_in.shape[1]
    pool_dim = n_in - ATTN_DIM - 2 * KV_DIM
    row = lambda bb, tt: (bb, tt, 0)
    in_specs = [_mods_spec(d),
                pl.BlockSpec((d, n_in), lambda bb, tt: (0, 0)),
                pl.BlockSpec((1, HEAD_DIM), lambda bb, tt: (0, 0)),
                pl.BlockSpec((1, HEAD_DIM), lambda bb, tt: (0, 0)),
                pl.BlockSpec((TM, HEAD_DIM), lambda bb, tt: (tt, 0)),
                pl.BlockSpec((TM, HEAD_DIM), lambda bb, tt: (tt, 0))]
    out_specs = [pl.BlockSpec((1, TM, ATTN_DIM), row),
                 pl.BlockSpec((1, N_KV_HEADS, HEAD_DIM, TM), lambda bb, tt: (bb, 0, 0, tt)),
                 pl.BlockSpec((1, TM, KV_DIM), row),
                 pl.BlockSpec((1, TM, pool_dim), row)]
    out_shape = [jax.ShapeDtypeStruct((b, l, ATTN_DIM), BF16),
                 jax.ShapeDtypeStruct((b, N_KV_HEADS, HEAD_DIM, l), BF16),
                 jax.ShapeDtypeStruct((b, l, KV_DIM), BF16),
                 jax.ShapeDtypeStruct((b, l, pool_dim), F32)]
    return pl.pallas_call(
        _attn_in_kernel, grid=(b, t),
        in_specs=[pl.BlockSpec((1, TM, d), row)] + in_specs,
        out_specs=out_specs, out_shape=out_shape,
        compiler_params=_cparams(("parallel", "parallel")), name="attn_in",
    )(x, mods, w_in, q_gain, k_gain, cos_t, sin_t)


def _attention_kernel(q_ref, kt_ref, v_ref, o_ref, sa_sc, sb_sc, m_sc, acc_sc, *, n_ctx, n_lat_chunks):
    t = pl.program_id(2)
    q = jnp.concatenate([q_ref[0, :, g * HEAD_DIM:(g + 1) * HEAD_DIM] for g in range(Q_PER_KV)],
                        axis=0)
    m_sc[...] = jnp.full(m_sc.shape, -1e30, F32)
    acc_sc[...] = jnp.zeros(acc_sc.shape, F32)

    def scores(start, size):
        return jnp.dot(q, kt_ref[0, 0, :, pl.ds(start, size)], preferred_element_type=F32)

    def v_ext(start, size):
        return jnp.concatenate([v_ref[0, pl.ds(start, size), :], jnp.ones((size, HEAD_DIM), BF16)],
                               axis=-1)

    def softmax_pv(load_s, vx, width):
        for g in range(Q_PER_KV):
            rows = slice(g * TM, (g + 1) * TM)
            s = load_s(rows)
            blocks = [s[:, c * LANES:(c + 1) * LANES] for c in range(width // LANES)]
            mc = blocks[0]
            for blk in blocks[1:]:
                mc = jnp.maximum(mc, blk)
            m_prev = m_sc[rows, :]
            m_new = jnp.maximum(m_prev, jnp.broadcast_to(jnp.max(mc, axis=-1, keepdims=True),
                                                         (TM, LANES)))
            alpha = jnp.exp2(m_prev - m_new)
            p = jnp.concatenate([jnp.exp2(blk - m_new).astype(BF16) for blk in blocks], axis=-1)
            pv = jnp.dot(p, vx, preferred_element_type=F32)
            acc_sc[rows, :HEAD_DIM] = alpha * acc_sc[rows, :HEAD_DIM] + pv[:, :HEAD_DIM]
            acc_sc[rows, HEAD_DIM:] = alpha * acc_sc[rows, HEAD_DIM:] + pv[:, HEAD_DIM:]
            m_sc[rows, :] = m_new

    s0 = scores(0, n_ctx)
    softmax_pv(lambda rows: s0[rows], v_ext(0, n_ctx), n_ctx)

    @pl.when(t > 0)
    def _():
        def start(c):
            return pl.multiple_of(n_ctx + c * ATT_CK, LANES)

        def pair(c, prefetch):
            sb_sc[...] = scores(start(c + 1), ATT_CK)
            softmax_pv(lambda rows: sa_sc[rows, :], v_ext(start(c), ATT_CK), ATT_CK)
            if prefetch:
                sa_sc[...] = scores(start(c + 2), ATT_CK)
            softmax_pv(lambda rows: sb_sc[rows, :], v_ext(start(c + 1), ATT_CK), ATT_CK)

        sa_sc[...] = scores(n_ctx, ATT_CK)

        def body(i, carry):
            pair(2 * i, True)
            return carry
        lax.fori_loop(0, n_lat_chunks // 2 - 1, body, 0)
        pair(n_lat_chunks - 2, False)

    for g in range(Q_PER_KV):
        rows = slice(g * TM, (g + 1) * TM)
        o = acc_sc[rows, :HEAD_DIM] / acc_sc[rows, HEAD_DIM:]
        o_ref[0, :, g * HEAD_DIM:(g + 1) * HEAD_DIM] = o.astype(BF16)


def _attention_call(q, k, v, n_ctx):
    b, l, _ = q.shape
    t = l // TM
    gw = Q_PER_KV * HEAD_DIM
    kern = functools.partial(_attention_kernel, n_ctx=n_ctx, n_lat_chunks=(l - n_ctx) // ATT_CK)
    return pl.pallas_call(
        kern,
        grid=(b, N_KV_HEADS, t),
        in_specs=[pl.BlockSpec((1, TM, gw), lambda bb, hh, tt: (bb, tt, hh)),
                  pl.BlockSpec((1, 1, HEAD_DIM, l), lambda bb, hh, tt: (bb, hh, 0, 0)),
                  pl.BlockSpec((1, l, HEAD_DIM), lambda bb, hh, tt: (bb, 0, hh))],
        out_specs=pl.BlockSpec((1, TM, gw), lambda bb, hh, tt: (bb, tt, hh)),
        out_shape=jax.ShapeDtypeStruct((b, l, ATTN_DIM), BF16),
        scratch_shapes=[pltpu.VMEM((Q_PER_KV * TM, ATT_CK), F32),
                        pltpu.VMEM((Q_PER_KV * TM, ATT_CK), F32),
                        pltpu.VMEM((Q_PER_KV * TM, LANES), F32),
                        pltpu.VMEM((Q_PER_KV * TM, 2 * HEAD_DIM), F32)],
        compiler_params=_cparams(("parallel", "parallel", "parallel")),
        name="attention",
    )(q, k, v)


ROUTE_ROWS = TM


def _residual_and_route(x_ref, mix_rows, mods, rw_ref, rb_ref, cnt_sc, xo_ref, hp_ref, meta_ref):
    first_step = (pl.program_id(0) == 0) & (pl.program_id(1) == 0)

    @pl.when(first_step)
    def _():
        cnt_sc[...] = jnp.zeros(cnt_sc.shape, F32)

    n_exp = rw_ref.shape[-1]
    R = ROUTE_ROWS
    lane = lax.broadcasted_iota(jnp.int32, (R, n_exp), 1).astype(F32)
    lane_m = lax.broadcasted_iota(jnp.int32, (R, LANES), 1)
    ri = lax.broadcasted_iota(jnp.int32, (R, R), 0)
    ci = lax.broadcasted_iota(jnp.int32, (R, R), 1)
    tri = jnp.where(ci < ri, 1.0, 0.0).astype(BF16)
    running = cnt_sc[...]
    for r0 in range(0, TM, R):
        rows = slice(r0, r0 + R)
        xnew = x_ref[0, rows, :] + mods[2:3, :] * mix_rows(r0)
        xo_ref[0, rows, :] = xnew
        h2 = _norm_mod(xnew, mods, 3, 4)
        hp_ref[rows, :] = _pack_bf16_pair(h2)

        h_hi = h2.astype(BF16)
        h_lo = (h2 - h_hi.astype(F32)).astype(BF16)
        logits = (jnp.dot(h_hi, rw_ref[0], preferred_element_type=F32)
                  + jnp.dot(h_lo, rw_ref[0], preferred_element_type=F32)
                  + jnp.dot(h_hi, rw_ref[1], preferred_element_type=F32)) + rb_ref[...]
        onehots, tops, idxs = [], [], []
        cur = logits
        for _ in range(TOP_K):
            m = jnp.max(cur, axis=-1, keepdims=True)
            idx = jnp.min(jnp.where(cur == m, lane, float(n_exp)), axis=-1, keepdims=True)
            sel = lane == idx
            tops.append(m)
            idxs.append(idx)
            onehots.append(sel)
            cur = jnp.where(sel, -jnp.inf, cur)
        ws = [jnp.exp(m - tops[0]) for m in tops]
        denom = ws[0] + ws[1] + ws[2] + ws[3]

        counts = jnp.zeros((R, n_exp), F32)
        for sel in onehots:
            counts = counts + jnp.where(sel, 1.0, 0.0)
        before = jnp.dot(tri, counts.astype(BF16), preferred_element_type=F32) + running
        running = running + jnp.sum(counts, axis=0, keepdims=True)

        meta = jnp.zeros((R, LANES), F32)
        for kk in range(TOP_K):
            rank = jnp.sum(jnp.where(onehots[kk], before, 0.0), axis=-1, keepdims=True)
            meta = jnp.where(lane_m == kk, idxs[kk], meta)
            meta = jnp.where(lane_m == TOP_K + kk, ws[kk] / denom, meta)
            meta = jnp.where(lane_m == 2 * TOP_K + kk, rank, meta)
        meta_ref[rows, :] = meta
    cnt_sc[...] = running
    return running


def _route_out_specs(t, d, n_exp):
    flat = lambda bb, tt: (bb * t + tt, 0)
    return [pl.BlockSpec((1, TM, d), lambda bb, tt: (bb, tt, 0)),
            pl.BlockSpec((TM, d // 2), flat),
            pl.BlockSpec((TM, LANES), flat),
            pl.BlockSpec((1, n_exp), lambda bb, tt: (0, 0))]


def _route_out_shapes(b, l, d, n_exp):
    return [jax.ShapeDtypeStruct((b, l, d), F32),
            jax.ShapeDtypeStruct((b * l, d // 2), U32),
            jax.ShapeDtypeStruct((b * l, LANES), F32),
            jax.ShapeDtypeStruct((1, n_exp), F32)]


def _attn_out_kernel(x_ref, mods_ref, ao_ref, p_ref, pp_ref, pn_ref, pw_ref, ps_ref,
                     woa_ref, wop_ref, rw_ref, rb_ref,
                     xo_ref, hp_ref, meta_ref, cnt_ref, e_sc, cnt_sc, *, n_ctx, n_lat):
    t = pl.program_id(1)
    nt = pl.num_programs(1)
    prev_ok = t >= 2
    next_ok = (t >= 1) & (t <= nt - 2)
    e_sc[0:HALO, :] = jnp.where(prev_ok, pp_ref[0], 0.0)
    e_sc[HALO:HALO + TM, :] = p_ref[0]
    e_sc[HALO + TM:, :] = jnp.where(next_ok, pn_ref[0], 0.0)

    seq_len = jnp.where(t == 0, n_ctx, n_lat)
    gw = p_ref.shape[-1] // N_POOL_GROUPS
    R = ROUTE_ROWS

    def mix_rows(r0):
        rowi = r0 + lax.broadcasted_iota(jnp.int32, (R, 1), 0)
        pos = jnp.where(t == 0, rowi, (t - 1) * TM + rowi)
        mix = jnp.dot(ao_ref[0, r0:r0 + R, :], woa_ref[...], preferred_element_type=F32)
        for g, w in enumerate(POOL_WINDOWS):
            ls = slice(g * gw, (g + 1) * gw)
            base = r0 + HALO - w // 2
            acc = e_sc[base:base + R, ls]
            for j in range(1, w):
                acc = acc + e_sc[base + j:base + j + R, ls]
            lo = jnp.maximum(pos - w // 2, 0)
            hi = jnp.minimum(pos + w - w // 2, seq_len)
            dlt = acc / (hi - lo).astype(F32) - e_sc[r0 + HALO:r0 + HALO + R, ls]
            y = jnp.dot(dlt.astype(BF16), pw_ref[g], preferred_element_type=F32) * ps_ref[:, ls]
            mix = mix + jnp.dot(y.astype(BF16), wop_ref[ls, :], preferred_element_type=F32)
        return mix

    cnt_ref[...] = _residual_and_route(x_ref, mix_rows, mods_ref[0, 0], rw_ref, rb_ref, cnt_sc,
                                       xo_ref, hp_ref, meta_ref)


def _attn_out_call(x, mods, ao, p, pool_w, pool_scale, wo_a, wo_p, rw, rb, n_ctx):
    b, l, d = x.shape
    t = l // TM
    pool_dim = p.shape[-1]
    n_exp = rw.shape[-1]
    nh = l // HALO
    per = TM // HALO
    row = lambda bb, tt: (bb, tt, 0)
    const2 = lambda bb, tt: (0, 0)
    kern = functools.partial(_attn_out_kernel, n_ctx=n_ctx, n_lat=l - n_ctx)
    return pl.pallas_call(
        kern,
        grid=(b, t),
        in_specs=[pl.BlockSpec((1, TM, d), row),
                  _mods_spec(d),
                  pl.BlockSpec((1, TM, ATTN_DIM), row),
                  pl.BlockSpec((1, TM, pool_dim), row),
                  pl.BlockSpec((1, HALO, pool_dim), lambda bb, tt: (bb, jnp.maximum(tt * per - 1, 0), 0)),
                  pl.BlockSpec((1, HALO, pool_dim), lambda bb, tt: (bb, jnp.minimum((tt + 1) * per, nh - 1), 0)),
                  pl.BlockSpec(pool_w.shape, lambda bb, tt: (0, 0, 0)),
                  pl.BlockSpec((1, pool_dim), const2),
                  pl.BlockSpec(wo_a.shape, const2),
                  pl.BlockSpec(wo_p.shape, const2),
                  pl.BlockSpec(rw.shape, lambda bb, tt: (0, 0, 0)),
                  pl.BlockSpec((1, n_exp), const2)],
        out_specs=_route_out_specs(t, d, n_exp),
        out_shape=_route_out_shapes(b, l, d, n_exp),
        scratch_shapes=[pltpu.VMEM((TM + 2 * HALO, pool_dim), F32),
                        pltpu.VMEM((1, n_exp), F32)],
        compiler_params=_cparams(("arbitrary", "arbitrary")),
        name="attn_out",
    )(x, mods, ao, p, p, p, pool_w, pool_scale, wo_a, wo_p, rw, rb)


def _gelu_tanh(x):
    return 0.5 * x * (1.0 + jnp.tanh(0.7978845608028654 * (x + 0.044715 * (x * x * x))))


SEG_LEN = TM // SUBLANES


def _segment_perm(to_segment_major):
    ri = lax.broadcasted_iota(jnp.int32, (TM, TM), 0)
    ci = lax.broadcasted_iota(jnp.int32, (TM, TM), 1)
    if to_segment_major:
        hit = ci == (ri % SUBLANES) * SEG_LEN + ri // SUBLANES
    else:
        hit = ci == (ri % SEG_LEN) * SUBLANES + ri // SEG_LEN
    return jnp.where(hit, 1.0, 0.0).astype(BF16)


def _rg_in_body(x, mods_ref, w_ref, o_ref, gelu):
    h = _norm_mod(x, mods_ref[0, 0], 0, 1).astype(BF16)
    h = jnp.dot(_segment_perm(True), h, preferred_element_type=F32).astype(BF16)
    z = jnp.dot(h, w_ref[...], preferred_element_type=F32)
    if gelu:
        z = _gelu_tanh(z)
    o_ref[0] = z.astype(o_ref.dtype)


def _rg_in_kernel(x_ref, mods_ref, w_ref, o_ref, *, gelu):
    _rg_in_body(x_ref[0], mods_ref, w_ref, o_ref, gelu)


def _rg_in_fused_kernel(dest_ref, dest_next_ref, meta_ref, xpre_ref, modsp_ref, ys_ref,
                        mods_ref, w_ref, xo_ref, o_ref, buf_a, buf_b, sems, *, gelu):
    def consume(x):
        xo_ref[0] = x
        _rg_in_body(x, mods_ref, w_ref, o_ref, gelu)

    _moe_gather_combine(dest_ref, dest_next_ref, meta_ref, xpre_ref, modsp_ref[0, 0][5:6, :],
                        ys_ref, buf_a, buf_b, sems, consume)


def _rg_in_call(x, mods, w, gelu, out_dtype, name, pending=None):
    b, l, d = (pending.x if x is None else x).shape
    t = l // TM
    n = w.shape[1]
    row = lambda bb, tt: (bb, tt, 0)
    in_specs = [_mods_spec(d), pl.BlockSpec((d, n), lambda bb, tt: (0, 0))]
    out_spec = pl.BlockSpec((1, TM, n), row)
    out_shape = jax.ShapeDtypeStruct((b, l, n), out_dtype)
    if pending is None:
        out = pl.pallas_call(
            functools.partial(_rg_in_kernel, gelu=gelu), grid=(b, t),
            in_specs=[pl.BlockSpec((1, TM, d), row)] + in_specs,
            out_specs=out_spec, out_shape=out_shape,
            compiler_params=_cparams(("parallel", "parallel")), name=name,
        )(x, mods, w)
        return x, out
    return pl.pallas_call(
        functools.partial(_rg_in_fused_kernel, gelu=gelu), grid=(b, t),
        in_specs=_pending_specs(b, t, d) + in_specs,
        out_specs=[pl.BlockSpec((1, TM, d), row), out_spec],
        out_shape=[jax.ShapeDtypeStruct((b, l, d), F32), out_shape],
        scratch_shapes=_pending_scratch(pending.ys.shape[-1]),
        compiler_params=_cparams(("arbitrary", "arbitrary")), name="combine_" + name,
    )(*_pending_args(pending), mods, w)


def _gate_slab_starts(d_rnn, blk):
    n_slabs = -(-d_rnn // GATE_TN)
    spans = []
    for j in range(n_slabs):
        c0, c1 = j * GATE_TN, min((j + 1) * GATE_TN, d_rnn)
        lo = (c0 // blk) * blk
        hi = ((c1 - 1) // blk + 1) * blk
        spans.append((lo // LANES * LANES, -(-hi // LANES) * LANES))
    km = max(h - l for l, h in spans)
    starts = [min(l, d_rnn - km) for l, _ in spans]
    return starts, km


def _gate_slabs(r_w, i_w, d_rnn):
    blk = r_w.shape[-1]
    starts, km = _gate_slab_starts(d_rnn, blk)
    n_slabs = len(starts)
    pad = n_slabs * GATE_TN - d_rnn
    eye = jnp.eye(r_w.shape[0], dtype=r_w.dtype)
    dense = [jnp.pad((w[:, :, None, :] * eye[:, None, :, None]).reshape(d_rnn, d_rnn),
                     ((0, 0), (0, pad))) for w in (r_w, i_w)]
    slabs = [jnp.concatenate([dd[s:s + km, j * GATE_TN:(j + 1) * GATE_TN] for dd in dense], axis=1)
             for j, s in enumerate(starts)]
    return jnp.stack(slabs).astype(BF16)


def _rg_scan_kernel(u_ref, up_ref, un_ref, cw_ref, cb_ref, w_ref, gb_ref, lam_ref, h_ref,
                    e_sc, ucb_sc, uc_sc, a_sc, b_sc, carry_sc, *, reverse, starts, km):
    s = pl.program_id(1)
    nt = pl.num_programs(1)
    tau = jnp.where(s == 0, 0, nt - s) if reverse else s
    d_rnn = u_ref.shape[-1]

    @pl.when(s == 0)
    def _():
        carry_sc[...] = jnp.zeros(carry_sc.shape, F32)

    prev_ok = tau >= 2
    next_ok = (tau >= 1) & (tau <= nt - 2)
    G = SUBLANES
    sub = lax.broadcasted_iota(jnp.int32, (G, d_rnn), 0)
    for j in range(CONV_LEFT):
        src = u_ref[0, TM - (CONV_LEFT - j) * G:TM - (CONV_LEFT - j - 1) * G, :]
        edge = jnp.where(prev_ok, up_ref[0, (j + 1) * G - 1:(j + 1) * G, :], 0.0)
        e_sc[j * G:(j + 1) * G, :] = jnp.where(sub == 0, edge, pltpu.roll(src, 1, 0))
    e_sc[CONV_LEFT * G:CONV_LEFT * G + TM, :] = u_ref[0]
    edge = jnp.where(next_ok, un_ref[0, 0:1, :], 0.0)
    e_sc[CONV_LEFT * G + TM:, :] = jnp.where(sub == G - 1, edge, pltpu.roll(u_ref[0, 0:G, :], G - 1, 0))
    cw = cw_ref[...]
    n_conv_chunks = d_rnn // (3 * LANES)
    for c in range(n_conv_chunks):
        ls = slice(c * 3 * LANES, (c + 1) * 3 * LANES)
        uc = cb_ref[:, ls] + cw[0:1, ls] * e_sc[0:TM, ls]
        for j in range(1, CONV_WIDTH):
            uc = uc + cw[j:j + 1, ls] * e_sc[j * G:j * G + TM, ls]
        uc_sc[:, ls] = uc
        ucb_sc[:, ls] = uc.astype(BF16)

    lam = lam_ref[0]
    neg = -lam
    softplus = jnp.maximum(neg, 0.0) + jnp.log(1.0 + jnp.exp(-jnp.abs(neg)))
    gb = gb_ref[0]
    for j, st in enumerate(starts):
        n = min(GATE_TN, d_rnn - j * GATE_TN)
        ls = slice(j * GATE_TN, j * GATE_TN + n)
        z = jnp.dot(ucb_sc[:, st:st + km], w_ref[0, j], preferred_element_type=F32)
        r = _sigmoid(z[:, :n] + gb[0:1, ls])
        i = _sigmoid(z[:, GATE_TN:GATE_TN + n] + gb[1:2, ls])
        log_a = (-RG_C) * r * softplus[:, ls]
        a = jnp.exp(log_a)
        mult = jnp.sqrt(jnp.maximum(1.0 - a * a, 0.0))
        b_sc[:, ls] = mult * (i * uc_sc[:, ls])
        a_sc[:, ls] = a

    steps = range(SEG_LEN - 1, -1, -1) if reverse else range(SEG_LEN)
    rowi = lax.broadcasted_iota(jnp.int32, (SUBLANES, SCAN_LANES), 0)
    grp = lambda t: slice(t * SUBLANES, (t + 1) * SUBLANES)
    for c in range(d_rnn // SCAN_LANES):
        ls = slice(c * SCAN_LANES, (c + 1) * SCAN_LANES)
        tot_h = jnp.zeros((SUBLANES, SCAN_LANES), F32)
        tot_p = jnp.ones((SUBLANES, SCAN_LANES), F32)
        for t in steps:
            a = a_sc[grp(t), ls]
            tot_h = a * tot_h + b_sc[grp(t), ls]
            tot_p = a * tot_p

        for sft in (1, 2, 4):
            if reverse:
                ok = rowi < SUBLANES - sft
                shift = SUBLANES - sft
            else:
                ok = rowi >= sft
                shift = sft
            pp = jnp.where(ok, pltpu.roll(tot_p, shift, 0), 1.0)
            hp = jnp.where(ok, pltpu.roll(tot_h, shift, 0), 0.0)
            tot_h = tot_p * hp + tot_h
            tot_p = tot_p * pp
        carry = carry_sc[:, ls]
        after = tot_p * carry + tot_h
        if reverse:
            h = jnp.where(rowi < SUBLANES - 1, pltpu.roll(after, SUBLANES - 1, 0), carry)
            carry_sc[:, ls] = jnp.broadcast_to(after[0:1, :], (SUBLANES, SCAN_LANES))
        else:
            h = jnp.where(rowi >= 1, pltpu.roll(after, 1, 0), carry)
            carry_sc[:, ls] = jnp.broadcast_to(after[SUBLANES - 1:SUBLANES, :], (SUBLANES, SCAN_LANES))

        for t in steps:
            h = a_sc[grp(t), ls] * h + b_sc[grp(t), ls]
            b_sc[grp(t), ls] = h

    h_ref[0] = b_sc[...].astype(h_ref.dtype)


def _rg_scan_call(u, conv_w, conv_b, slabs, gate_b, lam, reverse, blk):
    b, l, d_rnn = u.shape
    t = l // TM
    starts, km = _gate_slab_starts(d_rnn, blk)
    lead = CONV_LEFT * SUBLANES
    per_lead = TM // lead
    per_tail = TM // SUBLANES

    def tile(ss):
        return jnp.where(ss == 0, 0, t - ss) if reverse else ss

    kern = functools.partial(_rg_scan_kernel, reverse=reverse, starts=tuple(starts), km=km)
    const2 = lambda bb, ss: (0, 0)
    return pl.pallas_call(
        kern,
        grid=(b, t),
        in_specs=[pl.BlockSpec((1, TM, d_rnn), lambda bb, ss: (bb, tile(ss), 0)),
                  pl.BlockSpec((1, lead, d_rnn),
                               lambda bb, ss: (bb, jnp.maximum(tile(ss) * per_lead - 1, 0), 0)),
                  pl.BlockSpec((1, SUBLANES, d_rnn),
                               lambda bb, ss: (bb, jnp.minimum((tile(ss) + 1) * per_tail,
                                                               l // SUBLANES - 1), 0)),
                  pl.BlockSpec(conv_w.shape, const2),
                  pl.BlockSpec((1, d_rnn), const2),
                  pl.BlockSpec((1,) + slabs.shape, lambda bb, ss: (0, 0, 0, 0)),
                  pl.BlockSpec((1, 2, d_rnn), lambda bb, ss: (0, 0, 0)),
                  pl.BlockSpec((1, 1, d_rnn), lambda bb, ss: (0, 0, 0))],
        out_specs=pl.BlockSpec((1, TM, d_rnn), lambda bb, ss: (bb, tile(ss), 0)),
        out_shape=jax.ShapeDtypeStruct((b, l, d_rnn), BF16),
        scratch_shapes=[pltpu.VMEM((TM + (CONV_WIDTH - 1) * SUBLANES, d_rnn), F32),
                        pltpu.VMEM((TM, d_rnn), BF16),
                        pltpu.VMEM((TM, d_rnn), F32),
                        pltpu.VMEM((TM, d_rnn), F32),
                        pltpu.VMEM((TM, d_rnn), F32),
                        pltpu.VMEM((SUBLANES, d_rnn), F32)],
        compiler_params=_cparams(("arbitrary", "arbitrary")),
        name="rg_scan_bwd" if reverse else "rg_scan_fwd",
    )(u, u, u, conv_w, conv_b, slabs[None], gate_b[None], lam[None, None])


def _rg_out_kernel(x_ref, mods_ref, g_ref, hf_ref, hb_ref, wo_ref, rw_ref, rb_ref,
                   xo_ref, hp_ref, meta_ref, cnt_ref, cnt_sc):
    gated = g_ref[0].astype(F32) * (hf_ref[0].astype(F32) + hb_ref[0].astype(F32))
    perm = _segment_perm(False)
    gated = gated.astype(BF16)

    def mix_rows(r0):
        rows = jnp.dot(perm[r0:r0 + ROUTE_ROWS, :], gated, preferred_element_type=F32)
        return jnp.dot(rows.astype(BF16), wo_ref[...], preferred_element_type=F32)

    cnt_ref[...] = _residual_and_route(x_ref, mix_rows, mods_ref[0, 0], rw_ref, rb_ref, cnt_sc,
                                       xo_ref, hp_ref, meta_ref)


def _rg_out_call(x, mods, gg, hf, hb, w_out, rw, rb):
    b, l, d = x.shape
    t = l // TM
    d_rnn = gg.shape[-1]
    n_exp = rw.shape[-1]
    row = lambda bb, tt: (bb, tt, 0)
    const2 = lambda bb, tt: (0, 0)
    return pl.pallas_call(
        _rg_out_kernel,
        grid=(b, t),
        in_specs=[pl.BlockSpec((1, TM, d), row), _mods_spec(d),
                  pl.BlockSpec((1, TM, d_rnn), row),
                  pl.BlockSpec((1, TM, d_rnn), row),
                  pl.BlockSpec((1, TM, d_rnn), row),
                  pl.BlockSpec(w_out.shape, const2),
                  pl.BlockSpec(rw.shape, lambda bb, tt: (0, 0, 0)),
                  pl.BlockSpec((1, n_exp), const2)],
        out_specs=_route_out_specs(t, d, n_exp),
        out_shape=_route_out_shapes(b, l, d, n_exp),
        scratch_shapes=[pltpu.VMEM((1, n_exp), F32)],
        compiler_params=_cparams(("arbitrary", "arbitrary")),
        name="rg_out",
    )(x, mods, gg, hf, hb, w_out, rw, rb)


GU_ROWS = 512
EXPERT_ROWS = 256


def _gu_split_body(w_ref, o_ref):
    de = w_ref.shape[-1] // 2
    pair = 2 * LANES
    ri = lax.broadcasted_iota(jnp.int32, (pair, pair), 0)
    ci = lax.broadcasted_iota(jnp.int32, (pair, pair), 1)
    src = jnp.where(ci < LANES, 2 * ci, 2 * (ci - LANES) + 1)
    sel = jnp.where(ri == src, 1.0, 0.0).astype(BF16)
    for k in range(w_ref.shape[1]):
        w = w_ref[0, k].astype(BF16)
        for j in range(w.shape[1] // pair):
            r = jnp.dot(w[:, j * pair:(j + 1) * pair], sel, preferred_element_type=F32)
            o_ref[k, :, j * LANES:(j + 1) * LANES] = r[:, :LANES].astype(BF16)
            o_ref[k, :, de + j * LANES:de + (j + 1) * LANES] = r[:, LANES:].astype(BF16)


def _dispatch_kernel(pend_ref, padded_ref, dest_ref, hp_ref, w_ref, xs_ref, wgu_ref,
                     zero_sc, sem, zsem, *, n_blocks):
    step = pl.program_id(0)
    n_exp = pend_ref.shape[0]

    @pl.when(step == 0)
    def _():
        zero_sc[...] = jnp.zeros(zero_sc.shape, U32)

        def zero_block(row0):
            rows = pl.ds(pl.multiple_of(row0, MOE_BLK), MOE_BLK)
            return pltpu.make_async_copy(zero_sc, xs_ref.at[rows], zsem)

        first_tail = pend_ref[n_exp - 1] // MOE_BLK
        for go in (lambda cp: cp.start(), lambda cp: cp.wait()):
            def per_expert(e, carry, go=go):
                @pl.when(padded_ref[e] > 0)
                def _():
                    go(zero_block(pend_ref[e] - MOE_BLK))
                return carry
            lax.fori_loop(0, n_exp, per_expert, 0)

            def per_tail(i, carry, go=go):
                go(zero_block(i * MOE_BLK))
                return carry
            lax.fori_loop(first_tail, n_blocks, per_tail, 0)

    for r in range(TM):
        for kk in range(TOP_K):
            d = dest_ref[r * TOP_K + kk]
            pltpu.make_async_copy(hp_ref.at[pl.ds(r, 1)], xs_ref.at[pl.ds(d, 1)], sem).start()
    _gu_split_body(w_ref, wgu_ref)
    for _ in range(TOP_K):
        pltpu.make_async_copy(hp_ref, xs_ref.at[pl.ds(0, TM)], sem).wait()


def _dispatch_call(pend, padded, dest_flat, hp, n_blocks, w_gu_all, layer):
    n, c = hp.shape
    n_layers, n_exp, d, de2 = w_gu_all.shape
    n_wblk = n_exp * (d // GU_ROWS)
    kb = next(k for k in range(1, n_wblk + 1) if n_wblk % k == 0 and n_wblk // k <= n // TM)
    n_wsteps = n_wblk // kb

    def wstep(i):
        return jnp.minimum(i, n_wsteps - 1)

    kern = functools.partial(_dispatch_kernel, n_blocks=n_blocks)
    xs, wgu = pl.pallas_call(
        kern,
        grid_spec=pltpu.PrefetchScalarGridSpec(
            num_scalar_prefetch=2,
            grid=(n // TM,),
            in_specs=[pl.BlockSpec((TM * TOP_K,), lambda i, *_: (i,), memory_space=pltpu.SMEM),
                      pl.BlockSpec((TM, c), lambda i, *_: (i, 0)),
                      pl.BlockSpec((1, kb, GU_ROWS, de2), lambda i, *_: (layer, wstep(i), 0, 0))],
            out_specs=[pl.BlockSpec(memory_space=pl.ANY),
                       pl.BlockSpec((kb, GU_ROWS, de2), lambda i, *_: (wstep(i), 0, 0))],
            scratch_shapes=[pltpu.VMEM((MOE_BLK, c), U32),
                            pltpu.SemaphoreType.DMA(()),
                            pltpu.SemaphoreType.DMA(())]),
        out_shape=[jax.ShapeDtypeStruct((n_blocks * MOE_BLK, c), U32),
                   jax.ShapeDtypeStruct((n_wblk, GU_ROWS, de2), BF16)],
        compiler_params=pltpu.CompilerParams(dimension_semantics=("arbitrary",),
                                             vmem_limit_bytes=VMEM_LIMIT, has_side_effects=True),
        name="moe_dispatch",
    )(pend, padded, dest_flat, hp, w_gu_all.reshape(n_layers, n_wblk, GU_ROWS, de2))
    return xs, wgu.reshape(n_exp, d, de2)


def _expert_kernel(blk_e_ref, n_used_ref, xs_ref, wgu_ref, bgu_ref, wd_ref, bd_ref, ys_ref):
    used = pl.program_id(0) < n_used_ref[0]

    @pl.when(used)
    def _():
        for r0 in range(0, MOE_BLK, EXPERT_ROWS):
            rows = slice(r0, r0 + EXPERT_ROWS)
            w = xs_ref[rows, :]
            x = jnp.concatenate([_unpack_hi(w).astype(BF16), _unpack_lo(w).astype(BF16)], axis=-1)
            gu = jnp.dot(x, wgu_ref[0], preferred_element_type=F32) + bgu_ref[0]
            de = gu.shape[1] // 2
            g = jnp.minimum(gu[:, :de], SWIGLU_LIMIT)
            u = jnp.clip(gu[:, de:], -SWIGLU_LIMIT, SWIGLU_LIMIT)
            act = (u + 1.0) * (g * _sigmoid(SWIGLU_ALPHA * g))
            y = jnp.dot(act.astype(BF16), wd_ref[0], preferred_element_type=F32) + bd_ref[0]
            ys_ref[rows, :] = _pack_bf16_pair(y)

    @pl.when(jnp.logical_not(used))
    def _():
        ys_ref[...] = jnp.zeros(ys_ref.shape, U32)


def _expert_call(blk_e, n_used, xs, wgu, bgu, wd, bd):
    p_rows, c = xs.shape
    n_exp, d, de2 = wgu.shape
    de = de2 // 2
    by_e3 = lambda i, be, nu: (be[i], 0, 0)
    return pl.pallas_call(
        _expert_kernel,
        grid_spec=pltpu.PrefetchScalarGridSpec(
            num_scalar_prefetch=2,
            grid=(p_rows // MOE_BLK,),
            in_specs=[pl.BlockSpec((MOE_BLK, c), lambda i, be, nu: (i, 0)),
                      pl.BlockSpec((1, d, de2), by_e3),
                      pl.BlockSpec((1, 1, de2), by_e3),
                      pl.BlockSpec((1, de, d), by_e3),
                      pl.BlockSpec((1, 1, d), by_e3)],
            out_specs=pl.BlockSpec((MOE_BLK, c), lambda i, be, nu: (i, 0))),
        out_shape=jax.ShapeDtypeStruct((p_rows, c), U32),
        compiler_params=_cparams(("arbitrary",)),
        name="moe_experts",
    )(blk_e, n_used, xs, wgu, bgu, wd, bd)


def _moe_gather_combine(dest_ref, dest_next_ref, meta_ref, x_ref, gf, ys_ref, buf_a, buf_b, sems,
                        consume):
    step = pl.program_id(0) * pl.num_programs(1) + pl.program_id(1)
    n_steps = pl.num_programs(0) * pl.num_programs(1)
    c = buf_a.shape[-1]

    def request(idx_ref, buf, s):
        for r in range(TM):
            for kk in range(TOP_K):
                d = idx_ref[r * TOP_K + kk]
                pltpu.make_async_copy(ys_ref.at[pl.ds(d, 1)], buf.at[kk, pl.ds(r, 1)],
                                      sems.at[s]).start()

    def wait(buf, s):
        for kk in range(TOP_K):
            pltpu.make_async_copy(ys_ref.at[pl.ds(0, TM)], buf.at[kk], sems.at[s]).wait()

    @pl.when(step == 0)
    def _():
        request(dest_ref, buf_a, 0)

    def run(cur, s_cur, nxt, s_nxt):
        wait(cur, s_cur)
        request(dest_next_ref, nxt, s_nxt)
        meta = meta_ref[...]
        y_hi = jnp.zeros((TM, c), F32)
        y_lo = jnp.zeros((TM, c), F32)
        for kk in range(TOP_K):
            gk = meta[:, TOP_K + kk:TOP_K + kk + 1]
            w = cur[kk]
            y_hi = y_hi + gk * _unpack_hi(w)
            y_lo = y_lo + gk * _unpack_lo(w)
        consume(jnp.concatenate([x_ref[0, :, :c] + gf[:, :c] * y_hi,
                                 x_ref[0, :, c:] + gf[:, c:] * y_lo], axis=-1))

        @pl.when(step == n_steps - 1)
        def _():
            wait(nxt, s_nxt)

    @pl.when(step % 2 == 0)
    def _():
        run(buf_a, 0, buf_b, 1)

    @pl.when(step % 2 == 1)
    def _():
        run(buf_b, 1, buf_a, 0)


class _PendingCombine(NamedTuple):
    dest: jax.Array
    meta: jax.Array
    x: jax.Array
    mods: jax.Array
    ys: jax.Array


def _pending_specs(b, t, d):
    return [pl.BlockSpec((TM * TOP_K,), lambda bb, tt: (bb * t + tt,), memory_space=pltpu.SMEM),
            pl.BlockSpec((TM * TOP_K,), lambda bb, tt: (jnp.minimum(bb * t + tt + 1, b * t - 1),),
                         memory_space=pltpu.SMEM),
            pl.BlockSpec((TM, LANES), lambda bb, tt: (bb * t + tt, 0)),
            pl.BlockSpec((1, TM, d), lambda bb, tt: (bb, tt, 0)),
            _mods_spec(d),
            pl.BlockSpec(memory_space=pl.ANY)]


def _pending_scratch(c):
    return [pltpu.VMEM((TOP_K, TM, c), U32), pltpu.VMEM((TOP_K, TM, c), U32),
            pltpu.SemaphoreType.DMA((2,))]


def _pending_args(p):
    return (p.dest, p.dest, p.meta, p.x, p.mods, p.ys)


def _combine_kernel(dest_ref, dest_next_ref, meta_ref, x_ref, mods_ref, ys_ref, o_ref,
                    buf_a, buf_b, sems):
    def consume(x):
        o_ref[0] = x

    _moe_gather_combine(dest_ref, dest_next_ref, meta_ref, x_ref, mods_ref[0, 0][5:6, :],
                        ys_ref, buf_a, buf_b, sems, consume)


def _combine_call(pending):
    b, l, d = pending.x.shape
    t = l // TM
    return pl.pallas_call(
        _combine_kernel,
        grid=(b, t),
        in_specs=_pending_specs(b, t, d),
        out_specs=pl.BlockSpec((1, TM, d), lambda bb, tt: (bb, tt, 0)),
        out_shape=jax.ShapeDtypeStruct((b, l, d), F32),
        scratch_shapes=_pending_scratch(pending.ys.shape[-1]),
        compiler_params=_cparams(("arbitrary", "arbitrary")),
        name="moe_combine",
    )(*_pending_args(pending))


def _moe(x_new, hp, meta, counts, mods, w_gu_all, layer, bgu, wd, bd):
    n = hp.shape[0]
    n_exp = w_gu_all.shape[1]
    n_blocks = -(-n * TOP_K // MOE_BLK) + n_exp
    counts = counts.reshape(n_exp).astype(jnp.int32)
    padded = (counts + MOE_BLK - 1) // MOE_BLK * MOE_BLK
    pend = jnp.cumsum(padded).astype(jnp.int32)
    pstart = pend - padded
    top_e = meta[:, 0:TOP_K].astype(jnp.int32)
    rank = meta[:, 2 * TOP_K:3 * TOP_K].astype(jnp.int32)
    dest = (pstart[top_e] + rank).reshape(n * TOP_K)
    blk_row0 = jnp.arange(n_blocks, dtype=jnp.int32) * MOE_BLK
    blk_e = jnp.minimum(jnp.sum((pend[None, :] <= blk_row0[:, None]).astype(jnp.int32), axis=1),
                        n_exp - 1)
    xs, wgu = _dispatch_call(pend, padded, dest, hp, n_blocks, w_gu_all, layer)
    ys = _expert_call(blk_e, pend[n_exp - 1:] // MOE_BLK, xs, wgu, bgu, wd, bd)
    return _PendingCombine(dest, meta, x_new, mods, ys)


def _final_norm_kernel(x_ref, g_ref, o_ref):
    x = x_ref[0]
    o_ref[0] = x * lax.rsqrt(jnp.mean(x * x, axis=-1, keepdims=True) + EPS) * g_ref[...]


def _final_norm_call(x, gain, n_ctx):
    b, l, d = x.shape
    s = l - n_ctx
    skip = n_ctx // TM
    return pl.pallas_call(
        _final_norm_kernel,
        grid=(b, s // TM),
        in_specs=[pl.BlockSpec((1, TM, d), lambda bb, tt: (bb, tt + skip, 0)),
                  pl.BlockSpec((1, d), lambda bb, tt: (0, 0))],
        out_specs=pl.BlockSpec((1, TM, d), lambda bb, tt: (bb, tt, 0)),
        out_shape=jax.ShapeDtypeStruct((b, s, d), F32),
        compiler_params=_cparams(("parallel", "parallel")),
        name="final_norm",
    )(x, gain.reshape(1, d))


def _rope_tables(s, n_ctx):
    t = jnp.arange(s, dtype=jnp.int32)
    inv_freq = ROPE_THETA ** (-jnp.arange(ROPE_AXIS_DIM // 2, dtype=F32) * 2.0 / ROPE_AXIS_DIM)
    ang_r = (t // GRID_W).astype(F32)[:, None] * inv_freq
    ang_c = (t % GRID_W).astype(F32)[:, None] * inv_freq
    cr, sr, cc, sc = jnp.cos(ang_r), jnp.sin(ang_r), jnp.cos(ang_c), jnp.sin(ang_c)
    cos_t = jnp.concatenate([cr, cr, cc, cc], axis=-1)
    sin_t = jnp.concatenate([-sr, sr, -sc, sc], axis=-1)
    cos_t = jnp.concatenate([jnp.ones((n_ctx, HEAD_DIM), F32), cos_t], axis=0)
    sin_t = jnp.concatenate([jnp.zeros((n_ctx, HEAD_DIM), F32), sin_t], axis=0)
    return cos_t, sin_t


def kernel(x, c, ctx, c_ctx, mod_w, mod_b, ap_w_in, ap_q_gain, ap_k_gain, ap_pool_w, ap_pool_scale, ap_w_out, rg_w_in, rg_conv_w, rg_conv_b, rg_gate_r_w, rg_gate_r_b, rg_gate_i_w, rg_gate_i_b, rg_lambda, rg_w_out, moe_router_w, moe_router_b, moe_w_gu, moe_b_gu, moe_w_down, moe_b_down, final_norm_g):
    b, s, d = x.shape
    n_ctx = ctx.shape[1]
    depth = mod_w.shape[0]
    assert n_ctx % TM == 0 and s % TM == 0 and s % (2 * ATT_CK) == 0 and s % GRID_W == 0
    assert b + 1 <= SUBLANES and d % (2 * LANES) == 0

    cv = jnp.zeros((SUBLANES, d), F32).at[:b].set(c).at[b].set(c_ctx)
    mods_all = _mods_call(cv, mod_w, mod_b).reshape(depth, SUBLANES, 6, d)
    cos_t, sin_t = _rope_tables(s, n_ctx)

    xx = jnp.concatenate([ctx, x], axis=1)
    pending = None
    d_rnn = rg_w_out.shape[1]
    rnn_blk = rg_gate_r_w.shape[-1]
    assert d_rnn % SCAN_LANES == 0 and d_rnn % (3 * LANES) == 0

    for l in range(depth):
        i = l // 2
        ml = mods_all[l]
        mods = jnp.stack([jnp.broadcast_to(ml[b][None], (b, 6, d)), ml[:b]])
        rw_hi = moe_router_w[l].astype(BF16)
        rw = jnp.stack([rw_hi, (moe_router_w[l] - rw_hi.astype(F32)).astype(BF16)])
        rb = moe_router_b[l].reshape(1, -1)
        if l % 2 == 0:
            if pending is not None:
                xx = _combine_call(pending)
            q, k, v, p = _attn_in_call(xx, mods, ap_w_in[i].astype(BF16),
                                       ap_q_gain[i].reshape(1, -1), ap_k_gain[i].reshape(1, -1),
                                       cos_t, sin_t)
            ao = _attention_call(q, k, v, n_ctx)
            wo = ap_w_out[i].astype(BF16)
            x_new, hp, meta, counts = _attn_out_call(
                xx, mods, ao, p, ap_pool_w[i].astype(BF16), ap_pool_scale[i].reshape(1, -1),
                wo[:ATTN_DIM], wo[ATTN_DIM:], rw, rb, n_ctx)
        else:
            w_in = rg_w_in[i].astype(BF16)
            xx, gg = _rg_in_call(xx, mods, w_in[:, :d_rnn], True, BF16, "rg_in_gate", pending)
            _, u = _rg_in_call(xx, mods, w_in[:, d_rnn:], False, F32, "rg_in_u")
            hs = []
            for dr in range(2):
                slabs = _gate_slabs(rg_gate_r_w[i, dr], rg_gate_i_w[i, dr], d_rnn)
                gate_b = jnp.stack([rg_gate_r_b[i, dr], rg_gate_i_b[i, dr]])
                hs.append(_rg_scan_call(u, rg_conv_w[i], rg_conv_b[i].reshape(1, -1), slabs, gate_b,
                                        rg_lambda[i, dr], dr == 1, rnn_blk))
            x_new, hp, meta, counts = _rg_out_call(xx, mods, gg, hs[0], hs[1],
                                                   rg_w_out[i].astype(BF16), rw, rb)
        de = moe_w_down.shape[2]
        bgu = jnp.concatenate([moe_b_gu[l][:, 0::2], moe_b_gu[l][:, 1::2]], axis=-1)
        pending = _moe(x_new, hp, meta, counts, mods, moe_w_gu, l, bgu.reshape(-1, 1, 2 * de),
                       moe_w_down[l].astype(BF16), moe_b_down[l].reshape(-1, 1, d))
        xx = None

    return _final_norm_call(_combine_call(pending), final_norm_g, n_ctx)
```

```python
import functools
from typing import NamedTuple

import jax
import jax.numpy as jnp
import numpy as np
from jax import lax
from jax.experimental import pallas as pl
from jax.experimental.pallas import tpu as pltpu

F32 = jnp.float32
BF16 = jnp.bfloat16
U32 = jnp.uint32

EPS = 1e-6
GRID_W = 64
HEAD_DIM = 128
N_Q_HEADS = 12
N_KV_HEADS = 4
Q_PER_KV = N_Q_HEADS // N_KV_HEADS
ATTN_DIM = N_Q_HEADS * HEAD_DIM
KV_DIM = N_KV_HEADS * HEAD_DIM
ROPE_THETA = 10000.0
ROPE_AXIS_DIM = HEAD_DIM // 2
POOL_WINDOWS = (2, 4, 8, 16)
N_POOL_GROUPS = 4
N_RNN_BLOCKS = 16
CONV_WIDTH = 4
CONV_LEFT = 2
RG_C = 8.0
TOP_K = 4
SWIGLU_LIMIT = 7.0
SWIGLU_ALPHA = 1.702

LANES = 128
SUBLANES = 8
MXU_DIM = 256
VMEM_LIMIT = 56 * 1024 * 1024

TM = 256
HALO = SUBLANES
ATT_CK = 1024
LOG2E = 1.4426950408889634
MOE_BLK = 512
GATE_TN = 256
SCAN_LANES = 896


def _cparams(sem):
    return pltpu.CompilerParams(dimension_semantics=sem, vmem_limit_bytes=VMEM_LIMIT)


def _sigmoid(x):
    return 0.5 * jnp.tanh(0.5 * x) + 0.5


def _norm_mod(x, mods, i_shift, i_scale):
    ms = jnp.mean(x * x, axis=-1, keepdims=True)
    xn = x * lax.rsqrt(ms + EPS)
    return xn * (1.0 + mods[i_scale:i_scale + 1, :]) + mods[i_shift:i_shift + 1, :]


def _pack_bf16_pair(h):
    c = h.shape[-1] // 2
    bits = pltpu.bitcast(h.astype(BF16).astype(F32), U32)
    return (bits[:, :c] & jnp.uint32(0xFFFF0000)) | (bits[:, c:] >> jnp.uint32(16))


def _unpack_hi(w):
    return pltpu.bitcast(w & jnp.uint32(0xFFFF0000), F32)


def _unpack_lo(w):
    return pltpu.bitcast(w << jnp.uint32(16), F32)


def _mods_kernel(cv_ref, w_ref, b_ref, o_ref):
    cv = cv_ref[...]
    s = cv * _sigmoid(cv)
    o_ref[0] = jnp.dot(s.astype(BF16), w_ref[0].astype(BF16), preferred_element_type=F32) + b_ref[0]


def _mods_call(cv, mod_w, mod_b):
    depth, d, n6 = mod_w.shape
    tn = 1536
    return pl.pallas_call(
        _mods_kernel,
        grid=(depth, n6 // tn),
        in_specs=[pl.BlockSpec((SUBLANES, d), lambda l, j: (0, 0)),
                  pl.BlockSpec((1, d, tn), lambda l, j: (l, 0, j)),
                  pl.BlockSpec((1, 1, tn), lambda l, j: (l, 0, j))],
        out_specs=pl.BlockSpec((1, SUBLANES, tn), lambda l, j: (l, 0, j)),
        out_shape=jax.ShapeDtypeStruct((depth, SUBLANES, n6), F32),
        compiler_params=_cparams(("parallel", "parallel")),
        name="adaln_mods",
    )(cv, mod_w, mod_b.reshape(depth, 1, n6))


def _mods_spec(d):
    return pl.BlockSpec((1, 1, 6, d), lambda b, t: (jnp.minimum(t, 1), b, 0, 0))


def _attn_in_kernel(x_ref, mods_ref, w_ref, qg_ref, kg_ref, cos_ref, sin_ref,
                    q_ref, k_ref, v_ref, p_ref):
    h = _norm_mod(x_ref[0], mods_ref[0, 0], 0, 1).astype(BF16)
    z = jnp.dot(h, w_ref[...], preferred_element_type=F32)
    cos = cos_ref[...]
    sin = sin_ref[...]
    lane = lax.broadcasted_iota(jnp.int32, (TM, HEAD_DIM), 1)
    first = (lane % (ROPE_AXIS_DIM)) < (ROPE_AXIS_DIM // 2)

    def head(zh, gain, scale):
        n = zh * lax.rsqrt(jnp.mean(zh * zh, axis=-1, keepdims=True) + EPS) * gain
        half = ROPE_AXIS_DIM // 2
        swapped = jnp.where(first, pltpu.roll(n, HEAD_DIM - half, 1), pltpu.roll(n, half, 1))
        return (n * cos + swapped * sin) * scale

    qg = qg_ref[...]
    kg = kg_ref[...]
    for hh in range(N_Q_HEADS):
        sl = slice(hh * HEAD_DIM, (hh + 1) * HEAD_DIM)
        q_ref[0, :, sl] = head(z[:, sl], qg, LOG2E * HEAD_DIM ** -0.5).astype(BF16)
    for hh in range(N_KV_HEADS):
        zs = slice(ATTN_DIM + hh * HEAD_DIM, ATTN_DIM + (hh + 1) * HEAD_DIM)
        k_ref[0, hh] = head(z[:, zs], kg, 1.0).T.astype(BF16)
    v_ref[0] = z[:, ATTN_DIM + KV_DIM:ATTN_DIM + 2 * KV_DIM].astype(BF16)
    p_ref[0] = z[:, ATTN_DIM + 2 * KV_DIM:]


def _attn_in_call(x, mods, w_in, q_gain, k_gain, cos_t, sin_t):
    b, l, d = x.shape
    t = l // TM
    n_in = w_in.shape[1]
    pool_dim = n_in - ATTN_DIM - 2 * KV_DIM
    row = lambda bb, tt: (bb, tt, 0)
    in_specs = [_mods_spec(d),
                pl.BlockSpec((d, n_in), lambda bb, tt: (0, 0)),
                pl.BlockSpec((1, HEAD_DIM), lambda bb, tt: (0, 0)),
                pl.BlockSpec((1, HEAD_DIM), lambda bb, tt: (0, 0)),
                pl.BlockSpec((TM, HEAD_DIM), lambda bb, tt: (tt, 0)),
                pl.BlockSpec((TM, HEAD_DIM), lambda bb, tt: (tt, 0))]
    out_specs = [pl.BlockSpec((1, TM, ATTN_DIM), row),
                 pl.BlockSpec((1, N_KV_HEADS, HEAD_DIM, TM), lambda bb, tt: (bb, 0, 0, tt)),
                 pl.BlockSpec((1, TM, KV_DIM), row),
                 pl.BlockSpec((1, TM, pool_dim), row)]
    out_shape = [jax.ShapeDtypeStruct((b, l, ATTN_DIM), BF16),
                 jax.ShapeDtypeStruct((b, N_KV_HEADS, HEAD_DIM, l), BF16),
                 jax.ShapeDtypeStruct((b, l, KV_DIM), BF16),
                 jax.ShapeDtypeStruct((b, l, pool_dim), F32)]
    return pl.pallas_call(
        _attn_in_kernel, grid=(b, t),
        in_specs=[pl.BlockSpec((1, TM, d), row)] + in_specs,
        out_specs=out_specs, out_shape=out_shape,
        compiler_params=_cparams(("parallel", "parallel")), name="attn_in",
    )(x, mods, w_in, q_gain, k_gain, cos_t, sin_t)


def _attention_kernel(q_ref, kt_ref, v_ref, o_ref, sa_sc, sb_sc, m_sc, acc_sc, *, n_ctx, n_lat_chunks):
    t = pl.program_id(2)
    q = jnp.concatenate([q_ref[0, :, g * HEAD_DIM:(g + 1) * HEAD_DIM] for g in range(Q_PER_KV)],
                        axis=0)
    m_sc[...] = jnp.full(m_sc.shape, -1e30, F32)
    acc_sc[...] = jnp.zeros(acc_sc.shape, F32)

    def scores(start, size):
        return jnp.dot(q, kt_ref[0, 0, :, pl.ds(start, size)], preferred_element_type=F32)

    def v_ext(start, size):
        return jnp.concatenate([v_ref[0, pl.ds(start, size), :], jnp.ones((size, HEAD_DIM), BF16)],
                               axis=-1)

    def softmax_pv(load_s, vx, width):
        for g in range(Q_PER_KV):
            rows = slice(g * TM, (g + 1) * TM)
            s = load_s(rows)
            blocks = [s[:, c * LANES:(c + 1) * LANES] for c in range(width // LANES)]
            mc = blocks[0]
            for blk in blocks[1:]:
                mc = jnp.maximum(mc, blk)
            m_prev = m_sc[rows, :]
            m_new = jnp.maximum(m_prev, jnp.broadcast_to(jnp.max(mc, axis=-1, keepdims=True),
                                                         (TM, LANES)))
            alpha = jnp.exp2(m_prev - m_new)
            p = jnp.concatenate([jnp.exp2(blk - m_new).astype(BF16) for blk in blocks], axis=-1)
            pv = jnp.dot(p, vx, preferred_element_type=F32)
            acc_sc[rows, :HEAD_DIM] = alpha * acc_sc[rows, :HEAD_DIM] + pv[:, :HEAD_DIM]
            acc_sc[rows, HEAD_DIM:] = alpha * acc_sc[rows, HEAD_DIM:] + pv[:, HEAD_DIM:]
            m_sc[rows, :] = m_new

    s0 = scores(0, n_ctx)
    softmax_pv(lambda rows: s0[rows], v_ext(0, n_ctx), n_ctx)

    @pl.when(t > 0)
    def _():
        def start(c):
            return pl.multiple_of(n_ctx + c * ATT_CK, LANES)

        def pair(c, prefetch):
            sb_sc[...] = scores(start(c + 1), ATT_CK)
            softmax_pv(lambda rows: sa_sc[rows, :], v_ext(start(c), ATT_CK), ATT_CK)
            if prefetch:
                sa_sc[...] = scores(start(c + 2), ATT_CK)
            softmax_pv(lambda rows: sb_sc[rows, :], v_ext(start(c + 1), ATT_CK), ATT_CK)

        sa_sc[...] = scores(n_ctx, ATT_CK)

        def body(i, carry):
            pair(2 * i, True)
            return carry
        lax.fori_loop(0, n_lat_chunks // 2 - 1, body, 0)
        pair(n_lat_chunks - 2, False)

    for g in range(Q_PER_KV):
        rows = slice(g * TM, (g + 1) * TM)
        o = acc_sc[rows, :HEAD_DIM] / acc_sc[rows, HEAD_DIM:]
        o_ref[0, :, g * HEAD_DIM:(g + 1) * HEAD_DIM] = o.astype(BF16)


def _attention_call(q, k, v, n_ctx):
    b, l, _ = q.shape
    t = l // TM
    gw = Q_PER_KV * HEAD_DIM
    kern = functools.partial(_attention_kernel, n_ctx=n_ctx, n_lat_chunks=(l - n_ctx) // ATT_CK)
    return pl.pallas_call(
        kern,
        grid=(b, N_KV_HEADS, t),
        in_specs=[pl.BlockSpec((1, TM, gw), lambda bb, hh, tt: (bb, tt, hh)),
                  pl.BlockSpec((1, 1, HEAD_DIM, l), lambda bb, hh, tt: (bb, hh, 0, 0)),
                  pl.BlockSpec((1, l, HEAD_DIM), lambda bb, hh, tt: (bb, 0, hh))],
        out_specs=pl.BlockSpec((1, TM, gw), lambda bb, hh, tt: (bb, tt, hh)),
        out_shape=jax.ShapeDtypeStruct((b, l, ATTN_DIM), BF16),
        scratch_shapes=[pltpu.VMEM((Q_PER_KV * TM, ATT_CK), F32),
                        pltpu.VMEM((Q_PER_KV * TM, ATT_CK), F32),
                        pltpu.VMEM((Q_PER_KV * TM, LANES), F32),
                        pltpu.VMEM((Q_PER_KV * TM, 2 * HEAD_DIM), F32)],
        compiler_params=_cparams(("parallel", "parallel", "parallel")),
        name="attention",
    )(q, k, v)


ROUTE_ROWS = TM


def _residual_and_route(x_ref, mix_rows, mods, rw_ref, rb_ref, cnt_sc, xo_ref, hp_ref, meta_ref):
    first_step = (pl.program_id(0) == 0) & (pl.program_id(1) == 0)

    @pl.when(first_step)
    def _():
        cnt_sc[...] = jnp.zeros(cnt_sc.shape, F32)

    n_exp = rw_ref.shape[-1] // 2
    R = ROUTE_ROWS
    lane = lax.broadcasted_iota(jnp.int32, (R, n_exp), 1).astype(F32)
    lane_m = lax.broadcasted_iota(jnp.int32, (R, LANES), 1)
    ri = lax.broadcasted_iota(jnp.int32, (R, R), 0)
    ci = lax.broadcasted_iota(jnp.int32, (R, R), 1)
    tri = jnp.where(ci < ri, 1.0, 0.0).astype(BF16)
    running = cnt_sc[...]
    for r0 in range(0, TM, R):
        rows = slice(r0, r0 + R)
        xnew = x_ref[0, rows, :] + mods[2:3, :] * mix_rows(r0)
        xo_ref[0, rows, :] = xnew
        h2 = _norm_mod(xnew, mods, 3, 4)
        hp_ref[rows, :] = _pack_bf16_pair(h2)

        h_hi = h2.astype(BF16)
        h_lo = (h2 - h_hi.astype(F32)).astype(BF16)
        l_hi = jnp.dot(h_hi, rw_ref[...], preferred_element_type=F32)
        l_lo = jnp.dot(h_lo, rw_ref[...], preferred_element_type=F32)
        logits = l_hi[:, :n_exp] + l_hi[:, n_exp:] + l_lo[:, :n_exp] + rb_ref[...]
        onehots, tops, idxs = [], [], []
        cur = logits
        for _ in range(TOP_K):
            m = jnp.max(cur, axis=-1, keepdims=True)
            idx = jnp.min(jnp.where(cur == m, lane, float(n_exp)), axis=-1, keepdims=True)
            sel = lane == idx
            tops.append(m)
            idxs.append(idx)
            onehots.append(sel)
            cur = jnp.where(sel, -jnp.inf, cur)
        ws = [jnp.exp(m - tops[0]) for m in tops]
        denom = ws[0] + ws[1] + ws[2] + ws[3]

        counts = jnp.zeros((R, n_exp), F32)
        for sel in onehots:
            counts = counts + jnp.where(sel, 1.0, 0.0)
        before = jnp.dot(tri, counts.astype(BF16), preferred_element_type=F32) + running
        running = running + jnp.sum(counts, axis=0, keepdims=True)

        meta = jnp.zeros((R, LANES), F32)
        for kk in range(TOP_K):
            rank = jnp.sum(jnp.where(onehots[kk], before, 0.0), axis=-1, keepdims=True)
            meta = jnp.where(lane_m == kk, idxs[kk], meta)
            meta = jnp.where(lane_m == TOP_K + kk, ws[kk] / denom, meta)
            meta = jnp.where(lane_m == 2 * TOP_K + kk, rank, meta)
        meta_ref[rows, :] = meta
    cnt_sc[...] = running
    return running


def _route_out_specs(t, d, n_exp):
    flat = lambda bb, tt: (bb * t + tt, 0)
    return [pl.BlockSpec((1, TM, d), lambda bb, tt: (bb, tt, 0)),
            pl.BlockSpec((TM, d // 2), flat),
            pl.BlockSpec((TM, LANES), flat),
            pl.BlockSpec((1, n_exp), lambda bb, tt: (0, 0))]


def _route_out_shapes(b, l, d, n_exp):
    return [jax.ShapeDtypeStruct((b, l, d), F32),
            jax.ShapeDtypeStruct((b * l, d // 2), U32),
            jax.ShapeDtypeStruct((b * l, LANES), F32),
            jax.ShapeDtypeStruct((1, n_exp), F32)]


def _attn_out_kernel(x_ref, mods_ref, ao_ref, p_ref, pp_ref, pn_ref, pw_ref, ps_ref,
                     woa_ref, wop_ref, rw_ref, rb_ref,
                     xo_ref, hp_ref, meta_ref, cnt_ref, e_sc, cnt_sc, *, n_ctx, n_lat):
    t = pl.program_id(1)
    nt = pl.num_programs(1)
    prev_ok = t >= 2
    next_ok = (t >= 1) & (t <= nt - 2)
    e_sc[0:HALO, :] = jnp.where(prev_ok, pp_ref[0], 0.0)
    e_sc[HALO:HALO + TM, :] = p_ref[0]
    e_sc[HALO + TM:, :] = jnp.where(next_ok, pn_ref[0], 0.0)

    seq_len = jnp.where(t == 0, n_ctx, n_lat)
    gw = p_ref.shape[-1] // N_POOL_GROUPS
    R = ROUTE_ROWS

    def mix_rows(r0):
        rowi = r0 + lax.broadcasted_iota(jnp.int32, (R, 1), 0)
        pos = jnp.where(t == 0, rowi, (t - 1) * TM + rowi)
        mix = jnp.dot(ao_ref[0, r0:r0 + R, :], woa_ref[...], preferred_element_type=F32)
        for g, w in enumerate(POOL_WINDOWS):
            ls = slice(g * gw, (g + 1) * gw)
            base = r0 + HALO - w // 2
            acc = e_sc[base:base + R, ls]
            for j in range(1, w):
                acc = acc + e_sc[base + j:base + j + R, ls]
            lo = jnp.maximum(pos - w // 2, 0)
            hi = jnp.minimum(pos + w - w // 2, seq_len)
            dlt = acc / (hi - lo).astype(F32) - e_sc[r0 + HALO:r0 + HALO + R, ls]
            y = jnp.dot(dlt.astype(BF16), pw_ref[g], preferred_element_type=F32) * ps_ref[:, ls]
            mix = mix + jnp.dot(y.astype(BF16), wop_ref[ls, :], preferred_element_type=F32)
        return mix

    cnt_ref[...] = _residual_and_route(x_ref, mix_rows, mods_ref[0, 0], rw_ref, rb_ref, cnt_sc,
                                       xo_ref, hp_ref, meta_ref)


def _attn_out_call(x, mods, ao, p, pool_w, pool_scale, wo_a, wo_p, rw, rb, n_ctx):
    b, l, d = x.shape
    t = l // TM
    pool_dim = p.shape[-1]
    n_exp = rw.shape[-1] // 2
    nh = l // HALO
    per = TM // HALO
    row = lambda bb, tt: (bb, tt, 0)
    const2 = lambda bb, tt: (0, 0)
    kern = functools.partial(_attn_out_kernel, n_ctx=n_ctx, n_lat=l - n_ctx)
    return pl.pallas_call(
        kern,
        grid=(b, t),
        in_specs=[pl.BlockSpec((1, TM, d), row),
                  _mods_spec(d),
                  pl.BlockSpec((1, TM, ATTN_DIM), row),
                  pl.BlockSpec((1, TM, pool_dim), row),
                  pl.BlockSpec((1, HALO, pool_dim), lambda bb, tt: (bb, jnp.maximum(tt * per - 1, 0), 0)),
                  pl.BlockSpec((1, HALO, pool_dim), lambda bb, tt: (bb, jnp.minimum((tt + 1) * per, nh - 1), 0)),
                  pl.BlockSpec(pool_w.shape, lambda bb, tt: (0, 0, 0)),
                  pl.BlockSpec((1, pool_dim), const2),
                  pl.BlockSpec(wo_a.shape, const2),
                  pl.BlockSpec(wo_p.shape, const2),
                  pl.BlockSpec(rw.shape, lambda bb, tt: (0, 0)),
                  pl.BlockSpec((1, n_exp), const2)],
        out_specs=_route_out_specs(t, d, n_exp),
        out_shape=_route_out_shapes(b, l, d, n_exp),
        scratch_shapes=[pltpu.VMEM((TM + 2 * HALO, pool_dim), F32),
                        pltpu.VMEM((1, n_exp), F32)],
        compiler_params=_cparams(("arbitrary", "arbitrary")),
        name="attn_out",
    )(x, mods, ao, p, p, p, pool_w, pool_scale, wo_a, wo_p, rw, rb)


def _gelu_tanh(x):
    return 0.5 * x * (1.0 + jnp.tanh(0.7978845608028654 * (x + 0.044715 * (x * x * x))))


SEG_LEN = TM // SUBLANES


def _segment_perm(to_segment_major):
    ri = lax.broadcasted_iota(jnp.int32, (TM, TM), 0)
    ci = lax.broadcasted_iota(jnp.int32, (TM, TM), 1)
    if to_segment_major:
        hit = ci == (ri % SUBLANES) * SEG_LEN + ri // SUBLANES
    else:
        hit = ci == (ri % SEG_LEN) * SUBLANES + ri // SEG_LEN
    return jnp.where(hit, 1.0, 0.0).astype(BF16)


def _rg_in_body(x, mods_ref, w_ref, o_ref, gelu):
    h = _norm_mod(x, mods_ref[0, 0], 0, 1).astype(BF16)
    h = jnp.dot(_segment_perm(True), h, preferred_element_type=F32).astype(BF16)
    z = jnp.dot(h, w_ref[...], preferred_element_type=F32)
    if gelu:
        z = _gelu_tanh(z)
    o_ref[0] = z.astype(o_ref.dtype)


def _rg_in_kernel(x_ref, mods_ref, w_ref, o_ref, *, gelu):
    _rg_in_body(x_ref[0], mods_ref, w_ref, o_ref, gelu)


def _rg_in_fused_kernel(dest_ref, dest_next_ref, meta_ref, xpre_ref, modsp_ref, ys_ref,
                        mods_ref, w_ref, xo_ref, o_ref, buf_a, buf_b, sems, *, gelu):
    def consume(x):
        xo_ref[0] = x
        _rg_in_body(x, mods_ref, w_ref, o_ref, gelu)

    _moe_gather_combine(dest_ref, dest_next_ref, meta_ref, xpre_ref, modsp_ref[0, 0][5:6, :],
                        ys_ref, buf_a, buf_b, sems, consume)


def _rg_in_call(x, mods, w, gelu, out_dtype, name, pending=None):
    b, l, d = (pending.x if x is None else x).shape
    t = l // TM
    n = w.shape[1]
    row = lambda bb, tt: (bb, tt, 0)
    in_specs = [_mods_spec(d), pl.BlockSpec((d, n), lambda bb, tt: (0, 0))]
    out_spec = pl.BlockSpec((1, TM, n), row)
    out_shape = jax.ShapeDtypeStruct((b, l, n), out_dtype)
    if pending is None:
        out = pl.pallas_call(
            functools.partial(_rg_in_kernel, gelu=gelu), grid=(b, t),
            in_specs=[pl.BlockSpec((1, TM, d), row)] + in_specs,
            out_specs=out_spec, out_shape=out_shape,
            compiler_params=_cparams(("parallel", "parallel")), name=name,
        )(x, mods, w)
        return x, out
    return pl.pallas_call(
        functools.partial(_rg_in_fused_kernel, gelu=gelu), grid=(b, t),
        in_specs=_pending_specs(b, t, d) + in_specs,
        out_specs=[pl.BlockSpec((1, TM, d), row), out_spec],
        out_shape=[jax.ShapeDtypeStruct((b, l, d), F32), out_shape],
        scratch_shapes=_pending_scratch(pending.ys.shape[-1]),
        compiler_params=_cparams(("arbitrary", "arbitrary")), name="combine_" + name,
    )(*_pending_args(pending), mods, w)


def _gate_slab_starts(d_rnn, blk):
    n_slabs = -(-d_rnn // GATE_TN)
    spans = []
    for j in range(n_slabs):
        c0, c1 = j * GATE_TN, min((j + 1) * GATE_TN, d_rnn)
        lo = (c0 // blk) * blk
        hi = ((c1 - 1) // blk + 1) * blk
        spans.append((lo // LANES * LANES, -(-hi // LANES) * LANES))
    km = max(h - l for l, h in spans)
    starts = [min(l, d_rnn - km) for l, _ in spans]
    return starts, km


def _gate_slabs(r_w, i_w, d_rnn):
    blk = r_w.shape[-1]
    starts, km = _gate_slab_starts(d_rnn, blk)
    n_slabs = len(starts)
    pad = n_slabs * GATE_TN - d_rnn
    eye = jnp.eye(r_w.shape[0], dtype=r_w.dtype)
    dense = [jnp.pad((w[:, :, None, :] * eye[:, None, :, None]).reshape(d_rnn, d_rnn),
                     ((0, 0), (0, pad))) for w in (r_w, i_w)]
    slabs = [jnp.concatenate([dd[s:s + km, j * GATE_TN:(j + 1) * GATE_TN] for dd in dense], axis=1)
             for j, s in enumerate(starts)]
    return jnp.stack(slabs).astype(BF16)


def _rg_scan_kernel(u_ref, up_ref, un_ref, cw_ref, cb_ref, w_ref, gb_ref, lam_ref, h_ref,
                    e_sc, ucb_sc, uc_sc, a_sc, b_sc, carry_sc, *, reverse, starts, km):
    s = pl.program_id(1)
    nt = pl.num_programs(1)
    tau = jnp.where(s == 0, 0, nt - s) if reverse else s
    d_rnn = u_ref.shape[-1]

    @pl.when(s == 0)
    def _():
        carry_sc[...] = jnp.zeros(carry_sc.shape, F32)

    prev_ok = tau >= 2
    next_ok = (tau >= 1) & (tau <= nt - 2)
    G = SUBLANES
    sub = lax.broadcasted_iota(jnp.int32, (G, d_rnn), 0)
    for j in range(CONV_LEFT):
        src = u_ref[0, TM - (CONV_LEFT - j) * G:TM - (CONV_LEFT - j - 1) * G, :]
        edge = jnp.where(prev_ok, up_ref[0, (j + 1) * G - 1:(j + 1) * G, :], 0.0)
        e_sc[j * G:(j + 1) * G, :] = jnp.where(sub == 0, edge, pltpu.roll(src, 1, 0))
    e_sc[CONV_LEFT * G:CONV_LEFT * G + TM, :] = u_ref[0]
    edge = jnp.where(next_ok, un_ref[0, 0:1, :], 0.0)
    e_sc[CONV_LEFT * G + TM:, :] = jnp.where(sub == G - 1, edge, pltpu.roll(u_ref[0, 0:G, :], G - 1, 0))
    cw = cw_ref[...]
    n_conv_chunks = d_rnn // (3 * LANES)
    for c in range(n_conv_chunks):
        ls = slice(c * 3 * LANES, (c + 1) * 3 * LANES)
        uc = cb_ref[:, ls] + cw[0:1, ls] * e_sc[0:TM, ls]
        for j in range(1, CONV_WIDTH):
            uc = uc + cw[j:j + 1, ls] * e_sc[j * G:j * G + TM, ls]
        uc_sc[:, ls] = uc
        ucb_sc[:, ls] = uc.astype(BF16)

    lam = lam_ref[0]
    neg = -lam
    softplus = jnp.maximum(neg, 0.0) + jnp.log(1.0 + jnp.exp(-jnp.abs(neg)))
    gb = gb_ref[0]
    for j, st in enumerate(starts):
        n = min(GATE_TN, d_rnn - j * GATE_TN)
        ls = slice(j * GATE_TN, j * GATE_TN + n)
        z = jnp.dot(ucb_sc[:, st:st + km], w_ref[0, j], preferred_element_type=F32)
        r = _sigmoid(z[:, :n] + gb[0:1, ls])
        i = _sigmoid(z[:, GATE_TN:GATE_TN + n] + gb[1:2, ls])
        log_a = (-RG_C) * r * softplus[:, ls]
        a = jnp.exp(log_a)
        mult = jnp.sqrt(jnp.maximum(1.0 - a * a, 0.0))
        b_sc[:, ls] = mult * (i * uc_sc[:, ls])
        a_sc[:, ls] = a

    steps = range(SEG_LEN - 1, -1, -1) if reverse else range(SEG_LEN)
    rowi = lax.broadcasted_iota(jnp.int32, (SUBLANES, SCAN_LANES), 0)
    grp = lambda t: slice(t * SUBLANES, (t + 1) * SUBLANES)
    for c in range(d_rnn // SCAN_LANES):
        ls = slice(c * SCAN_LANES, (c + 1) * SCAN_LANES)
        tot_h = jnp.zeros((SUBLANES, SCAN_LANES), F32)
        tot_p = jnp.ones((SUBLANES, SCAN_LANES), F32)
        for t in steps:
            a = a_sc[grp(t), ls]
            tot_h = a * tot_h + b_sc[grp(t), ls]
            tot_p = a * tot_p

        for sft in (1, 2, 4):
            if reverse:
                ok = rowi < SUBLANES - sft
                shift = SUBLANES - sft
            else:
                ok = rowi >= sft
                shift = sft
            pp = jnp.where(ok, pltpu.roll(tot_p, shift, 0), 1.0)
            hp = jnp.where(ok, pltpu.roll(tot_h, shift, 0), 0.0)
            tot_h = tot_p * hp + tot_h
            tot_p = tot_p * pp
        carry = carry_sc[:, ls]
        after = tot_p * carry + tot_h
        if reverse:
            h = jnp.where(rowi < SUBLANES - 1, pltpu.roll(after, SUBLANES - 1, 0), carry)
            carry_sc[:, ls] = jnp.broadcast_to(after[0:1, :], (SUBLANES, SCAN_LANES))
        else:
            h = jnp.where(rowi >= 1, pltpu.roll(after, 1, 0), carry)
            carry_sc[:, ls] = jnp.broadcast_to(after[SUBLANES - 1:SUBLANES, :], (SUBLANES, SCAN_LANES))

        for t in steps:
            h = a_sc[grp(t), ls] * h + b_sc[grp(t), ls]
            b_sc[grp(t), ls] = h

    h_ref[0] = b_sc[...].astype(h_ref.dtype)


def _rg_scan_call(u, conv_w, conv_b, slabs, gate_b, lam, reverse, blk):
    b, l, d_rnn = u.shape
    t = l // TM
    starts, km = _gate_slab_starts(d_rnn, blk)
    lead = CONV_LEFT * SUBLANES
    per_lead = TM // lead
    per_tail = TM // SUBLANES

    def tile(ss):
        return jnp.where(ss == 0, 0, t - ss) if reverse else ss

    kern = functools.partial(_rg_scan_kernel, reverse=reverse, starts=tuple(starts), km=km)
    const2 = lambda bb, ss: (0, 0)
    return pl.pallas_call(
        kern,
        grid=(b, t),
        in_specs=[pl.BlockSpec((1, TM, d_rnn), lambda bb, ss: (bb, tile(ss), 0)),
                  pl.BlockSpec((1, lead, d_rnn),
                               lambda bb, ss: (bb, jnp.maximum(tile(ss) * per_lead - 1, 0), 0)),
                  pl.BlockSpec((1, SUBLANES, d_rnn),
                               lambda bb, ss: (bb, jnp.minimum((tile(ss) + 1) * per_tail,
                                                               l // SUBLANES - 1), 0)),
                  pl.BlockSpec(conv_w.shape, const2),
                  pl.BlockSpec((1, d_rnn), const2),
                  pl.BlockSpec((1,) + slabs.shape, lambda bb, ss: (0, 0, 0, 0)),
                  pl.BlockSpec((1, 2, d_rnn), lambda bb, ss: (0, 0, 0)),
                  pl.BlockSpec((1, 1, d_rnn), lambda bb, ss: (0, 0, 0))],
        out_specs=pl.BlockSpec((1, TM, d_rnn), lambda bb, ss: (bb, tile(ss), 0)),
        out_shape=jax.ShapeDtypeStruct((b, l, d_rnn), BF16),
        scratch_shapes=[pltpu.VMEM((TM + (CONV_WIDTH - 1) * SUBLANES, d_rnn), F32),
                        pltpu.VMEM((TM, d_rnn), BF16),
                        pltpu.VMEM((TM, d_rnn), F32),
                        pltpu.VMEM((TM, d_rnn), F32),
                        pltpu.VMEM((TM, d_rnn), F32),
                        pltpu.VMEM((SUBLANES, d_rnn), F32)],
        compiler_params=_cparams(("arbitrary", "arbitrary")),
        name="rg_scan_bwd" if reverse else "rg_scan_fwd",
    )(u, u, u, conv_w, conv_b, slabs[None], gate_b[None], lam[None, None])


def _rg_out_kernel(x_ref, mods_ref, g_ref, hf_ref, hb_ref, wo_ref, rw_ref, rb_ref,
                   xo_ref, hp_ref, meta_ref, cnt_ref, cnt_sc):
    gated = g_ref[0].astype(F32) * (hf_ref[0].astype(F32) + hb_ref[0].astype(F32))
    perm = _segment_perm(False)
    gated = gated.astype(BF16)

    def mix_rows(r0):
        rows = jnp.dot(perm[r0:r0 + ROUTE_ROWS, :], gated, preferred_element_type=F32)
        return jnp.dot(rows.astype(BF16), wo_ref[...], preferred_element_type=F32)

    cnt_ref[...] = _residual_and_route(x_ref, mix_rows, mods_ref[0, 0], rw_ref, rb_ref, cnt_sc,
                                       xo_ref, hp_ref, meta_ref)


def _rg_out_call(x, mods, gg, hf, hb, w_out, rw, rb):
    b, l, d = x.shape
    t = l // TM
    d_rnn = gg.shape[-1]
    n_exp = rw.shape[-1] // 2
    row = lambda bb, tt: (bb, tt, 0)
    const2 = lambda bb, tt: (0, 0)
    return pl.pallas_call(
        _rg_out_kernel,
        grid=(b, t),
        in_specs=[pl.BlockSpec((1, TM, d), row), _mods_spec(d),
                  pl.BlockSpec((1, TM, d_rnn), row),
                  pl.BlockSpec((1, TM, d_rnn), row),
                  pl.BlockSpec((1, TM, d_rnn), row),
                  pl.BlockSpec(w_out.shape, const2),
                  pl.BlockSpec(rw.shape, lambda bb, tt: (0, 0)),
                  pl.BlockSpec((1, n_exp), const2)],
        out_specs=_route_out_specs(t, d, n_exp),
        out_shape=_route_out_shapes(b, l, d, n_exp),
        scratch_shapes=[pltpu.VMEM((1, n_exp), F32)],
        compiler_params=_cparams(("arbitrary", "arbitrary")),
        name="rg_out",
    )(x, mods, gg, hf, hb, w_out, rw, rb)


GU_ROWS = 512
EXPERT_ROWS = 256


def _gu_split_body(w_ref, o_ref):
    de = w_ref.shape[-1] // 2
    pair = 2 * LANES
    ri = lax.broadcasted_iota(jnp.int32, (pair, pair), 0)
    ci = lax.broadcasted_iota(jnp.int32, (pair, pair), 1)
    src = jnp.where(ci < LANES, 2 * ci, 2 * (ci - LANES) + 1)
    sel = jnp.where(ri == src, 1.0, 0.0).astype(BF16)
    for k in range(w_ref.shape[1]):
        w = w_ref[0, k].astype(BF16)
        for j in range(w.shape[1] // pair):
            r = jnp.dot(w[:, j * pair:(j + 1) * pair], sel, preferred_element_type=F32)
            o_ref[k, :, j * LANES:(j + 1) * LANES] = r[:, :LANES].astype(BF16)
            o_ref[k, :, de + j * LANES:de + (j + 1) * LANES] = r[:, LANES:].astype(BF16)


def _dispatch_kernel(pend_ref, padded_ref, dest_ref, hp_ref, w_ref, xs_ref, wgu_ref,
                     zero_sc, sem, zsem, *, n_blocks):
    step = pl.program_id(0)
    n_exp = pend_ref.shape[0]

    @pl.when(step == 0)
    def _():
        zero_sc[...] = jnp.zeros(zero_sc.shape, U32)

        def zero_block(row0):
            rows = pl.ds(pl.multiple_of(row0, MOE_BLK), MOE_BLK)
            return pltpu.make_async_copy(zero_sc, xs_ref.at[rows], zsem)

        first_tail = pend_ref[n_exp - 1] // MOE_BLK
        for go in (lambda cp: cp.start(), lambda cp: cp.wait()):
            def per_expert(e, carry, go=go):
                @pl.when(padded_ref[e] > 0)
                def _():
                    go(zero_block(pend_ref[e] - MOE_BLK))
                return carry
            lax.fori_loop(0, n_exp, per_expert, 0)

            def per_tail(i, carry, go=go):
                go(zero_block(i * MOE_BLK))
                return carry
            lax.fori_loop(first_tail, n_blocks, per_tail, 0)

    for r in range(TM):
        for kk in range(TOP_K):
            d = dest_ref[r * TOP_K + kk]
            pltpu.make_async_copy(hp_ref.at[pl.ds(r, 1)], xs_ref.at[pl.ds(d, 1)], sem).start()
    _gu_split_body(w_ref, wgu_ref)
    for _ in range(TOP_K):
        pltpu.make_async_copy(hp_ref, xs_ref.at[pl.ds(0, TM)], sem).wait()


def _dispatch_call(pend, padded, dest_flat, hp, n_blocks, w_gu_all, layer):
    n, c = hp.shape
    n_layers, n_exp, d, de2 = w_gu_all.shape
    n_wblk = n_exp * (d // GU_ROWS)
    kb = next(k for k in range(1, n_wblk + 1) if n_wblk % k == 0 and n_wblk // k <= n // TM)
    n_wsteps = n_wblk // kb

    def wstep(i):
        return jnp.minimum(i, n_wsteps - 1)

    kern = functools.partial(_dispatch_kernel, n_blocks=n_blocks)
    xs, wgu = pl.pallas_call(
        kern,
        grid_spec=pltpu.PrefetchScalarGridSpec(
            num_scalar_prefetch=2,
            grid=(n // TM,),
            in_specs=[pl.BlockSpec((TM * TOP_K,), lambda i, *_: (i,), memory_space=pltpu.SMEM),
                      pl.BlockSpec((TM, c), lambda i, *_: (i, 0)),
                      pl.BlockSpec((1, kb, GU_ROWS, de2), lambda i, *_: (layer, wstep(i), 0, 0))],
            out_specs=[pl.BlockSpec(memory_space=pl.ANY),
                       pl.BlockSpec((kb, GU_ROWS, de2), lambda i, *_: (wstep(i), 0, 0))],
            scratch_shapes=[pltpu.VMEM((MOE_BLK, c), U32),
                            pltpu.SemaphoreType.DMA(()),
                            pltpu.SemaphoreType.DMA(())]),
        out_shape=[jax.ShapeDtypeStruct((n_blocks * MOE_BLK, c), U32),
                   jax.ShapeDtypeStruct((n_wblk, GU_ROWS, de2), BF16)],
        compiler_params=pltpu.CompilerParams(dimension_semantics=("arbitrary",),
                                             vmem_limit_bytes=VMEM_LIMIT, has_side_effects=True),
        name="moe_dispatch",
    )(pend, padded, dest_flat, hp, w_gu_all.reshape(n_layers, n_wblk, GU_ROWS, de2))
    return xs, wgu


def _expert_kernel(blk_e_ref, n_used_ref, xs_ref, wgu_ref, bgu_ref, wd_ref, bd_ref, ys_ref):
    used = pl.program_id(0) < n_used_ref[0]

    @pl.when(used)
    def _():
        for r0 in range(0, MOE_BLK, EXPERT_ROWS):
            rows = slice(r0, r0 + EXPERT_ROWS)
            w = xs_ref[rows, :]
            x = jnp.concatenate([_unpack_hi(w).astype(BF16), _unpack_lo(w).astype(BF16)], axis=-1)
            gu = bgu_ref[0]
            for k in range(wgu_ref.shape[0]):
                gu = gu + jnp.dot(x[:, k * GU_ROWS:(k + 1) * GU_ROWS], wgu_ref[k],
                                  preferred_element_type=F32)
            de = gu.shape[1] // 2
            g = jnp.minimum(gu[:, :de], SWIGLU_LIMIT)
            u = jnp.clip(gu[:, de:], -SWIGLU_LIMIT, SWIGLU_LIMIT)
            act = (u + 1.0) * (g * _sigmoid(SWIGLU_ALPHA * g))
            y = jnp.dot(act.astype(BF16), wd_ref[0], preferred_element_type=F32) + bd_ref[0]
            ys_ref[rows, :] = _pack_bf16_pair(y)

    @pl.when(jnp.logical_not(used))
    def _():
        ys_ref[...] = jnp.zeros(ys_ref.shape, U32)


def _expert_call(blk_e, n_used, xs, wgu, bgu, wd, bd):
    p_rows, c = xs.shape
    n_exp, de, d = wd.shape
    de2 = 2 * de
    per_e = d // GU_ROWS
    by_e3 = lambda i, be, nu: (be[i], 0, 0)
    return pl.pallas_call(
        _expert_kernel,
        grid_spec=pltpu.PrefetchScalarGridSpec(
            num_scalar_prefetch=2,
            grid=(p_rows // MOE_BLK,),
            in_specs=[pl.BlockSpec((MOE_BLK, c), lambda i, be, nu: (i, 0)),
                      pl.BlockSpec((per_e, GU_ROWS, de2), by_e3),
                      pl.BlockSpec((1, 1, de2), by_e3),
                      pl.BlockSpec((1, de, d), by_e3),
                      pl.BlockSpec((1, 1, d), by_e3)],
            out_specs=pl.BlockSpec((MOE_BLK, c), lambda i, be, nu: (i, 0))),
        out_shape=jax.ShapeDtypeStruct((p_rows, c), U32),
        compiler_params=_cparams(("arbitrary",)),
        name="moe_experts",
    )(blk_e, n_used, xs, wgu, bgu, wd, bd)


def _moe_gather_combine(dest_ref, dest_next_ref, meta_ref, x_ref, gf, ys_ref, buf_a, buf_b, sems,
                        consume):
    step = pl.program_id(0) * pl.num_programs(1) + pl.program_id(1)
    n_steps = pl.num_programs(0) * pl.num_programs(1)
    c = buf_a.shape[-1]

    def request(idx_ref, buf, s):
        for r in range(TM):
            for kk in range(TOP_K):
                d = idx_ref[r * TOP_K + kk]
                pltpu.make_async_copy(ys_ref.at[pl.ds(d, 1)], buf.at[kk, pl.ds(r, 1)],
                                      sems.at[s]).start()

    def wait(buf, s):
        for kk in range(TOP_K):
            pltpu.make_async_copy(ys_ref.at[pl.ds(0, TM)], buf.at[kk], sems.at[s]).wait()

    @pl.when(step == 0)
    def _():
        request(dest_ref, buf_a, 0)

    def run(cur, s_cur, nxt, s_nxt):
        wait(cur, s_cur)
        request(dest_next_ref, nxt, s_nxt)
        meta = meta_ref[...]
        y_hi = jnp.zeros((TM, c), F32)
        y_lo = jnp.zeros((TM, c), F32)
        for kk in range(TOP_K):
            gk = meta[:, TOP_K + kk:TOP_K + kk + 1]
            w = cur[kk]
            y_hi = y_hi + gk * _unpack_hi(w)
            y_lo = y_lo + gk * _unpack_lo(w)
        consume(jnp.concatenate([x_ref[0, :, :c] + gf[:, :c] * y_hi,
                                 x_ref[0, :, c:] + gf[:, c:] * y_lo], axis=-1))

        @pl.when(step == n_steps - 1)
        def _():
            wait(nxt, s_nxt)

    @pl.when(step % 2 == 0)
    def _():
        run(buf_a, 0, buf_b, 1)

    @pl.when(step % 2 == 1)
    def _():
        run(buf_b, 1, buf_a, 0)


class _PendingCombine(NamedTuple):
    dest: jax.Array
    meta: jax.Array
    x: jax.Array
    mods: jax.Array
    ys: jax.Array


def _pending_specs(b, t, d):
    return [pl.BlockSpec((TM * TOP_K,), lambda bb, tt: (bb * t + tt,), memory_space=pltpu.SMEM),
            pl.BlockSpec((TM * TOP_K,), lambda bb, tt: (jnp.minimum(bb * t + tt + 1, b * t - 1),),
                         memory_space=pltpu.SMEM),
            pl.BlockSpec((TM, LANES), lambda bb, tt: (bb * t + tt, 0)),
            pl.BlockSpec((1, TM, d), lambda bb, tt: (bb, tt, 0)),
            _mods_spec(d),
            pl.BlockSpec(memory_space=pl.ANY)]


def _pending_scratch(c):
    return [pltpu.VMEM((TOP_K, TM, c), U32), pltpu.VMEM((TOP_K, TM, c), U32),
            pltpu.SemaphoreType.DMA((2,))]


def _pending_args(p):
    return (p.dest, p.dest, p.meta, p.x, p.mods, p.ys)


def _combine_kernel(dest_ref, dest_next_ref, meta_ref, x_ref, mods_ref, ys_ref, o_ref,
                    buf_a, buf_b, sems):
    def consume(x):
        o_ref[0] = x

    _moe_gather_combine(dest_ref, dest_next_ref, meta_ref, x_ref, mods_ref[0, 0][5:6, :],
                        ys_ref, buf_a, buf_b, sems, consume)


def _combine_call(pending):
    b, l, d = pending.x.shape
    t = l // TM
    return pl.pallas_call(
        _combine_kernel,
        grid=(b, t),
        in_specs=_pending_specs(b, t, d),
        out_specs=pl.BlockSpec((1, TM, d), lambda bb, tt: (bb, tt, 0)),
        out_shape=jax.ShapeDtypeStruct((b, l, d), F32),
        scratch_shapes=_pending_scratch(pending.ys.shape[-1]),
        compiler_params=_cparams(("arbitrary", "arbitrary")),
        name="moe_combine",
    )(*_pending_args(pending))


def _moe(x_new, hp, meta, counts, mods, w_gu_all, layer, bgu, wd, bd):
    n = hp.shape[0]
    n_exp = w_gu_all.shape[1]
    n_blocks = -(-n * TOP_K // MOE_BLK) + n_exp
    counts = counts.reshape(n_exp).astype(jnp.int32)
    padded = (counts + MOE_BLK - 1) // MOE_BLK * MOE_BLK
    pend = jnp.cumsum(padded).astype(jnp.int32)
    pstart = pend - padded
    top_e = meta[:, 0:TOP_K].astype(jnp.int32)
    rank = meta[:, 2 * TOP_K:3 * TOP_K].astype(jnp.int32)
    dest = (pstart[top_e] + rank).reshape(n * TOP_K)
    blk_row0 = jnp.arange(n_blocks, dtype=jnp.int32) * MOE_BLK
    blk_e = jnp.minimum(jnp.sum((pend[None, :] <= blk_row0[:, None]).astype(jnp.int32), axis=1),
                        n_exp - 1)
    xs, wgu = _dispatch_call(pend, padded, dest, hp, n_blocks, w_gu_all, layer)
    ys = _expert_call(blk_e, pend[n_exp - 1:] // MOE_BLK, xs, wgu, bgu, wd, bd)
    return _PendingCombine(dest, meta, x_new, mods, ys)


def _final_norm_kernel(x_ref, g_ref, o_ref):
    x = x_ref[0]
    o_ref[0] = x * lax.rsqrt(jnp.mean(x * x, axis=-1, keepdims=True) + EPS) * g_ref[...]


def _final_norm_call(x, gain, n_ctx):
    b, l, d = x.shape
    s = l - n_ctx
    skip = n_ctx // TM
    return pl.pallas_call(
        _final_norm_kernel,
        grid=(b, s // TM),
        in_specs=[pl.BlockSpec((1, TM, d), lambda bb, tt: (bb, tt + skip, 0)),
                  pl.BlockSpec((1, d), lambda bb, tt: (0, 0))],
        out_specs=pl.BlockSpec((1, TM, d), lambda bb, tt: (bb, tt, 0)),
        out_shape=jax.ShapeDtypeStruct((b, s, d), F32),
        compiler_params=_cparams(("parallel", "parallel")),
        name="final_norm",
    )(x, gain.reshape(1, d))


def _rope_tables(s, n_ctx):
    t = jnp.arange(s, dtype=jnp.int32)
    inv_freq = ROPE_THETA ** (-jnp.arange(ROPE_AXIS_DIM // 2, dtype=F32) * 2.0 / ROPE_AXIS_DIM)
    ang_r = (t // GRID_W).astype(F32)[:, None] * inv_freq
    ang_c = (t % GRID_W).astype(F32)[:, None] * inv_freq
    cr, sr, cc, sc = jnp.cos(ang_r), jnp.sin(ang_r), jnp.cos(ang_c), jnp.sin(ang_c)
    cos_t = jnp.concatenate([cr, cr, cc, cc], axis=-1)
    sin_t = jnp.concatenate([-sr, sr, -sc, sc], axis=-1)
    cos_t = jnp.concatenate([jnp.ones((n_ctx, HEAD_DIM), F32), cos_t], axis=0)
    sin_t = jnp.concatenate([jnp.zeros((n_ctx, HEAD_DIM), F32), sin_t], axis=0)
    return cos_t, sin_t


def kernel(x, c, ctx, c_ctx, mod_w, mod_b, ap_w_in, ap_q_gain, ap_k_gain, ap_pool_w, ap_pool_scale, ap_w_out, rg_w_in, rg_conv_w, rg_conv_b, rg_gate_r_w, rg_gate_r_b, rg_gate_i_w, rg_gate_i_b, rg_lambda, rg_w_out, moe_router_w, moe_router_b, moe_w_gu, moe_b_gu, moe_w_down, moe_b_down, final_norm_g):
    b, s, d = x.shape
    n_ctx = ctx.shape[1]
    depth = mod_w.shape[0]
    assert n_ctx % TM == 0 and s % TM == 0 and s % (2 * ATT_CK) == 0 and s % GRID_W == 0
    assert b + 1 <= SUBLANES and d % (2 * LANES) == 0

    cv = jnp.zeros((SUBLANES, d), F32).at[:b].set(c).at[b].set(c_ctx)
    mods_all = _mods_call(cv, mod_w, mod_b).reshape(depth, SUBLANES, 6, d)
    cos_t, sin_t = _rope_tables(s, n_ctx)

    xx = jnp.concatenate([ctx, x], axis=1)
    pending = None
    d_rnn = rg_w_out.shape[1]
    rnn_blk = rg_gate_r_w.shape[-1]
    assert d_rnn % SCAN_LANES == 0 and d_rnn % (3 * LANES) == 0

    for l in range(depth):
        i = l // 2
        ml = mods_all[l]
        mods = jnp.stack([jnp.broadcast_to(ml[b][None], (b, 6, d)), ml[:b]])
        rw_hi = moe_router_w[l].astype(BF16)
        rw = jnp.concatenate([rw_hi, (moe_router_w[l] - rw_hi.astype(F32)).astype(BF16)],
                             axis=1)
        rb = moe_router_b[l].reshape(1, -1)
        if l % 2 == 0:
            if pending is not None:
                xx = _combine_call(pending)
            q, k, v, p = _attn_in_call(xx, mods, ap_w_in[i].astype(BF16),
                                       ap_q_gain[i].reshape(1, -1), ap_k_gain[i].reshape(1, -1),
                                       cos_t, sin_t)
            ao = _attention_call(q, k, v, n_ctx)
            wo = ap_w_out[i].astype(BF16)
            x_new, hp, meta, counts = _attn_out_call(
                xx, mods, ao, p, ap_pool_w[i].astype(BF16), ap_pool_scale[i].reshape(1, -1),
                wo[:ATTN_DIM], wo[ATTN_DIM:], rw, rb, n_ctx)
        else:
            w_in = rg_w_in[i].astype(BF16)
            xx, gg = _rg_in_call(xx, mods, w_in[:, :d_rnn], True, BF16, "rg_in_gate", pending)
            _, u = _rg_in_call(xx, mods, w_in[:, d_rnn:], False, F32, "rg_in_u")
            hs = []
            for dr in range(2):
                slabs = _gate_slabs(rg_gate_r_w[i, dr], rg_gate_i_w[i, dr], d_rnn)
                gate_b = jnp.stack([rg_gate_r_b[i, dr], rg_gate_i_b[i, dr]])
                hs.append(_rg_scan_call(u, rg_conv_w[i], rg_conv_b[i].reshape(1, -1), slabs, gate_b,
                                        rg_lambda[i, dr], dr == 1, rnn_blk))
            x_new, hp, meta, counts = _rg_out_call(xx, mods, gg, hs[0], hs[1],
                                                   rg_w_out[i].astype(BF16), rw, rb)
        de = moe_w_down.shape[2]
        bgu = jnp.concatenate([moe_b_gu[l][:, 0::2], moe_b_gu[l][:, 1::2]], axis=-1)
        pending = _moe(x_new, hp, meta, counts, mods, moe_w_gu, l, bgu.reshape(-1, 1, 2 * de),
                       moe_w_down[l].astype(BF16), moe_b_down[l].reshape(-1, 1, d))
        xx = None

    return _final_norm_call(_combine_call(pending), final_norm_g, n_ctx)
```

```python
import functools
from typing import NamedTuple

import jax
import jax.numpy as jnp
import numpy as np
from jax import lax
from jax.experimental import pallas as pl
from jax.experimental.pallas import tpu as pltpu

F32 = jnp.float32
BF16 = jnp.bfloat16
U32 = jnp.uint32

EPS = 1e-6
GRID_W = 64
HEAD_DIM = 128
N_Q_HEADS = 12
N_KV_HEADS = 4
Q_PER_KV = N_Q_HEADS // N_KV_HEADS
ATTN_DIM = N_Q_HEADS * HEAD_DIM
KV_DIM = N_KV_HEADS * HEAD_DIM
ROPE_THETA = 10000.0
ROPE_AXIS_DIM = HEAD_DIM // 2
POOL_WINDOWS = (2, 4, 8, 16)
N_POOL_GROUPS = 4
N_RNN_BLOCKS = 16
CONV_WIDTH = 4
CONV_LEFT = 2
RG_C = 8.0
TOP_K = 4
SWIGLU_LIMIT = 7.0
SWIGLU_ALPHA = 1.702

LANES = 128
SUBLANES = 8
MXU_DIM = 256
VMEM_LIMIT = 56 * 1024 * 1024

TM = 256
HALO = SUBLANES
ATT_CK = 1024
LOG2E = 1.4426950408889634
MOE_BLK = 512
GATE_TN = 256
SCAN_LANES = 896


def _cparams(sem):
    return pltpu.CompilerParams(dimension_semantics=sem, vmem_limit_bytes=VMEM_LIMIT)


def _sigmoid(x):
    return 0.5 * jnp.tanh(0.5 * x) + 0.5


def _norm_mod(x, mods, i_shift, i_scale):
    ms = jnp.mean(x * x, axis=-1, keepdims=True)
    xn = x * lax.rsqrt(ms + EPS)
    return xn * (1.0 + mods[i_scale:i_scale + 1, :]) + mods[i_shift:i_shift + 1, :]


def _pack_bf16_pair(h):
    c = h.shape[-1] // 2
    bits = pltpu.bitcast(h.astype(BF16).astype(F32), U32)
    return (bits[:, :c] & jnp.uint32(0xFFFF0000)) | (bits[:, c:] >> jnp.uint32(16))


def _unpack_hi(w):
    return pltpu.bitcast(w & jnp.uint32(0xFFFF0000), F32)


def _unpack_lo(w):
    return pltpu.bitcast(w << jnp.uint32(16), F32)


def _mods_kernel(cv_ref, w_ref, b_ref, o_ref):
    cv = cv_ref[...]
    s = cv * _sigmoid(cv)
    o_ref[0] = jnp.dot(s.astype(BF16), w_ref[0].astype(BF16), preferred_element_type=F32) + b_ref[0]


def _mods_call(cv, mod_w, mod_b):
    depth, d, n6 = mod_w.shape
    tn = 1536
    return pl.pallas_call(
        _mods_kernel,
        grid=(depth, n6 // tn),
        in_specs=[pl.BlockSpec((SUBLANES, d), lambda l, j: (0, 0)),
                  pl.BlockSpec((1, d, tn), lambda l, j: (l, 0, j)),
                  pl.BlockSpec((1, 1, tn), lambda l, j: (l, 0, j))],
        out_specs=pl.BlockSpec((1, SUBLANES, tn), lambda l, j: (l, 0, j)),
        out_shape=jax.ShapeDtypeStruct((depth, SUBLANES, n6), F32),
        compiler_params=_cparams(("parallel", "parallel")),
        name="adaln_mods",
    )(cv, mod_w, mod_b.reshape(depth, 1, n6))


def _mods_spec(d):
    return pl.BlockSpec((1, 1, 6, d), lambda b, t: (jnp.minimum(t, 1), b, 0, 0))


def _attn_in_kernel(x_ref, mods_ref, w_ref, qg_ref, kg_ref, cos_ref, sin_ref,
                    q_ref, k_ref, v_ref, p_ref):
    h = _norm_mod(x_ref[0], mods_ref[0, 0], 0, 1).astype(BF16)
    z = jnp.dot(h, w_ref[...], preferred_element_type=F32)
    cos = cos_ref[...]
    sin = sin_ref[...]
    lane = lax.broadcasted_iota(jnp.int32, (TM, HEAD_DIM), 1)
    first = (lane % (ROPE_AXIS_DIM)) < (ROPE_AXIS_DIM // 2)

    def head(zh, gain, scale):
        n = zh * lax.rsqrt(jnp.mean(zh * zh, axis=-1, keepdims=True) + EPS) * gain
        half = ROPE_AXIS_DIM // 2
        swapped = jnp.where(first, pltpu.roll(n, HEAD_DIM - half, 1), pltpu.roll(n, half, 1))
        return (n * cos + swapped * sin) * scale

    qg = qg_ref[...]
    kg = kg_ref[...]
    for hh in range(N_Q_HEADS):
        sl = slice(hh * HEAD_DIM, (hh + 1) * HEAD_DIM)
        q_ref[0, :, sl] = head(z[:, sl], qg, LOG2E * HEAD_DIM ** -0.5).astype(BF16)
    for hh in range(N_KV_HEADS):
        zs = slice(ATTN_DIM + hh * HEAD_DIM, ATTN_DIM + (hh + 1) * HEAD_DIM)
        k_ref[0, hh] = head(z[:, zs], kg, 1.0).T.astype(BF16)
    v_ref[0] = z[:, ATTN_DIM + KV_DIM:ATTN_DIM + 2 * KV_DIM].astype(BF16)
    p_ref[0] = z[:, ATTN_DIM + 2 * KV_DIM:]


def _attn_in_call(x, mods, w_in, q_gain, k_gain, cos_t, sin_t):
    b, l, d = x.shape
    t = l // TM
    n_in = w_in.shape[1]
    pool_dim = n_in - ATTN_DIM - 2 * KV_DIM
    row = lambda bb, tt: (bb, tt, 0)
    in_specs = [_mods_spec(d),
                pl.BlockSpec((d, n_in), lambda bb, tt: (0, 0)),
                pl.BlockSpec((1, HEAD_DIM), lambda bb, tt: (0, 0)),
                pl.BlockSpec((1, HEAD_DIM), lambda bb, tt: (0, 0)),
                pl.BlockSpec((TM, HEAD_DIM), lambda bb, tt: (tt, 0)),
                pl.BlockSpec((TM, HEAD_DIM), lambda bb, tt: (tt, 0))]
    out_specs = [pl.BlockSpec((1, TM, ATTN_DIM), row),
                 pl.BlockSpec((1, N_KV_HEADS, HEAD_DIM, TM), lambda bb, tt: (bb, 0, 0, tt)),
                 pl.BlockSpec((1, TM, KV_DIM), row),
                 pl.BlockSpec((1, TM, pool_dim), row)]
    out_shape = [jax.ShapeDtypeStruct((b, l, ATTN_DIM), BF16),
                 jax.ShapeDtypeStruct((b, N_KV_HEADS, HEAD_DIM, l), BF16),
                 jax.ShapeDtypeStruct((b, l, KV_DIM), BF16),
                 jax.ShapeDtypeStruct((b, l, pool_dim), F32)]
    return pl.pallas_call(
        _attn_in_kernel, grid=(b, t),
        in_specs=[pl.BlockSpec((1, TM, d), row)] + in_specs,
        out_specs=out_specs, out_shape=out_shape,
        compiler_params=_cparams(("parallel", "parallel")), name="attn_in",
    )(x, mods, w_in, q_gain, k_gain, cos_t, sin_t)


def _attention_kernel(q_ref, kt_ref, v_ref, o_ref, sa_sc, sb_sc, m_sc, acc_sc, *, n_ctx, n_lat_chunks):
    t = pl.program_id(2)
    q = jnp.concatenate([q_ref[0, :, g * HEAD_DIM:(g + 1) * HEAD_DIM] for g in range(Q_PER_KV)],
                        axis=0)
    m_sc[...] = jnp.full(m_sc.shape, -1e30, F32)
    acc_sc[...] = jnp.zeros(acc_sc.shape, F32)

    def scores(start, size):
        return jnp.dot(q, kt_ref[0, 0, :, pl.ds(start, size)], preferred_element_type=F32)

    def v_ext(start, size):
        return jnp.concatenate([v_ref[0, pl.ds(start, size), :], jnp.ones((size, HEAD_DIM), BF16)],
                               axis=-1)

    def softmax_pv(load_s, vx, width):
        for g in range(Q_PER_KV):
            rows = slice(g * TM, (g + 1) * TM)
            s = load_s(rows)
            blocks = [s[:, c * LANES:(c + 1) * LANES] for c in range(width // LANES)]
            mc = blocks[0]
            for blk in blocks[1:]:
                mc = jnp.maximum(mc, blk)
            m_prev = m_sc[rows, :]
            m_new = jnp.maximum(m_prev, jnp.broadcast_to(jnp.max(mc, axis=-1, keepdims=True),
                                                         (TM, LANES)))
            alpha = jnp.exp2(m_prev - m_new)
            p = jnp.concatenate([jnp.exp2(blk - m_new).astype(BF16) for blk in blocks], axis=-1)
            pv = jnp.dot(p, vx, preferred_element_type=F32)
            acc_sc[rows, :HEAD_DIM] = alpha * acc_sc[rows, :HEAD_DIM] + pv[:, :HEAD_DIM]
            acc_sc[rows, HEAD_DIM:] = alpha * acc_sc[rows, HEAD_DIM:] + pv[:, HEAD_DIM:]
            m_sc[rows, :] = m_new

    s0 = scores(0, n_ctx)
    softmax_pv(lambda rows: s0[rows], v_ext(0, n_ctx), n_ctx)

    @pl.when(t > 0)
    def _():
        def start(c):
            return pl.multiple_of(n_ctx + c * ATT_CK, LANES)

        def pair(c, prefetch):
            sb_sc[...] = scores(start(c + 1), ATT_CK)
            softmax_pv(lambda rows: sa_sc[rows, :], v_ext(start(c), ATT_CK), ATT_CK)
            if prefetch:
                sa_sc[...] = scores(start(c + 2), ATT_CK)
            softmax_pv(lambda rows: sb_sc[rows, :], v_ext(start(c + 1), ATT_CK), ATT_CK)

        sa_sc[...] = scores(n_ctx, ATT_CK)

        def body(i, carry):
            pair(2 * i, True)
            return carry
        lax.fori_loop(0, n_lat_chunks // 2 - 1, body, 0)
        pair(n_lat_chunks - 2, False)

    for g in range(Q_PER_KV):
        rows = slice(g * TM, (g + 1) * TM)
        o = acc_sc[rows, :HEAD_DIM] / acc_sc[rows, HEAD_DIM:]
        o_ref[0, :, g * HEAD_DIM:(g + 1) * HEAD_DIM] = o.astype(BF16)


def _attention_call(q, k, v, n_ctx):
    b, l, _ = q.shape
    t = l // TM
    gw = Q_PER_KV * HEAD_DIM
    kern = functools.partial(_attention_kernel, n_ctx=n_ctx, n_lat_chunks=(l - n_ctx) // ATT_CK)
    return pl.pallas_call(
        kern,
        grid=(b, N_KV_HEADS, t),
        in_specs=[pl.BlockSpec((1, TM, gw), lambda bb, hh, tt: (bb, tt, hh)),
                  pl.BlockSpec((1, 1, HEAD_DIM, l), lambda bb, hh, tt: (bb, hh, 0, 0)),
                  pl.BlockSpec((1, l, HEAD_DIM), lambda bb, hh, tt: (bb, 0, hh))],
        out_specs=pl.BlockSpec((1, TM, gw), lambda bb, hh, tt: (bb, tt, hh)),
        out_shape=jax.ShapeDtypeStruct((b, l, ATTN_DIM), BF16),
        scratch_shapes=[pltpu.VMEM((Q_PER_KV * TM, ATT_CK), F32),
                        pltpu.VMEM((Q_PER_KV * TM, ATT_CK), F32),
                        pltpu.VMEM((Q_PER_KV * TM, LANES), F32),
                        pltpu.VMEM((Q_PER_KV * TM, 2 * HEAD_DIM), F32)],
        compiler_params=_cparams(("parallel", "parallel", "parallel")),
        name="attention",
    )(q, k, v)


ROUTE_ROWS = TM


def _residual_and_route(x_ref, mix_rows, mods, rw_ref, rb_ref, cnt_sc, xo_ref, hp_ref, meta_ref):
    first_step = (pl.program_id(0) == 0) & (pl.program_id(1) == 0)

    @pl.when(first_step)
    def _():
        cnt_sc[...] = jnp.zeros(cnt_sc.shape, F32)

    n_exp = rw_ref.shape[-1] // 2
    R = ROUTE_ROWS
    lane = lax.broadcasted_iota(jnp.int32, (R, n_exp), 1).astype(F32)
    lane_m = lax.broadcasted_iota(jnp.int32, (R, LANES), 1)
    ri = lax.broadcasted_iota(jnp.int32, (R, R), 0)
    ci = lax.broadcasted_iota(jnp.int32, (R, R), 1)
    tri = jnp.where(ci < ri, 1.0, 0.0).astype(BF16)
    running = cnt_sc[...]
    for r0 in range(0, TM, R):
        rows = slice(r0, r0 + R)
        xnew = x_ref[0, rows, :] + mods[2:3, :] * mix_rows(r0)
        xo_ref[0, rows, :] = xnew
        h2 = _norm_mod(xnew, mods, 3, 4)
        hp_ref[rows, :] = _pack_bf16_pair(h2)

        h_hi = h2.astype(BF16)
        h_lo = (h2 - h_hi.astype(F32)).astype(BF16)
        l_hi = jnp.dot(h_hi, rw_ref[...], preferred_element_type=F32)
        l_lo = jnp.dot(h_lo, rw_ref[...], preferred_element_type=F32)
        logits = l_hi[:, :n_exp] + l_hi[:, n_exp:] + l_lo[:, :n_exp] + rb_ref[...]
        onehots, tops, idxs = [], [], []
        cur = logits
        for _ in range(TOP_K):
            m = jnp.max(cur, axis=-1, keepdims=True)
            idx = jnp.min(jnp.where(cur == m, lane, float(n_exp)), axis=-1, keepdims=True)
            sel = lane == idx
            tops.append(m)
            idxs.append(idx)
            onehots.append(sel)
            cur = jnp.where(sel, -jnp.inf, cur)
        ws = [jnp.exp(m - tops[0]) for m in tops]
        denom = ws[0] + ws[1] + ws[2] + ws[3]

        counts = jnp.zeros((R, n_exp), F32)
        for sel in onehots:
            counts = counts + jnp.where(sel, 1.0, 0.0)
        before = jnp.dot(tri, counts.astype(BF16), preferred_element_type=F32) + running
        running = running + jnp.sum(counts, axis=0, keepdims=True)

        meta = jnp.zeros((R, LANES), F32)
        for kk in range(TOP_K):
            rank = jnp.sum(jnp.where(onehots[kk], before, 0.0), axis=-1, keepdims=True)
            meta = jnp.where(lane_m == kk, idxs[kk], meta)
            meta = jnp.where(lane_m == TOP_K + kk, ws[kk] / denom, meta)
            meta = jnp.where(lane_m == 2 * TOP_K + kk, rank, meta)
        meta_ref[rows, :] = meta
    cnt_sc[...] = running
    return running


def _route_out_specs(t, d, n_exp):
    flat = lambda bb, tt: (bb * t + tt, 0)
    return [pl.BlockSpec((1, TM, d), lambda bb, tt: (bb, tt, 0)),
            pl.BlockSpec((TM, d // 2), flat),
            pl.BlockSpec((TM, LANES), flat),
            pl.BlockSpec((1, n_exp), lambda bb, tt: (0, 0))]


def _route_out_shapes(b, l, d, n_exp):
    return [jax.ShapeDtypeStruct((b, l, d), F32),
            jax.ShapeDtypeStruct((b * l, d // 2), U32),
            jax.ShapeDtypeStruct((b * l, LANES), F32),
            jax.ShapeDtypeStruct((1, n_exp), F32)]


def _attn_out_kernel(x_ref, mods_ref, ao_ref, p_ref, pp_ref, pn_ref, pw_ref, ps_ref,
                     woa_ref, wop_ref, rw_ref, rb_ref,
                     xo_ref, hp_ref, meta_ref, cnt_ref, e_sc, cnt_sc, *, n_ctx, n_lat):
    t = pl.program_id(1)
    nt = pl.num_programs(1)
    prev_ok = t >= 2
    next_ok = (t >= 1) & (t <= nt - 2)
    e_sc[0:HALO, :] = jnp.where(prev_ok, pp_ref[0], 0.0)
    e_sc[HALO:HALO + TM, :] = p_ref[0]
    e_sc[HALO + TM:, :] = jnp.where(next_ok, pn_ref[0], 0.0)

    seq_len = jnp.where(t == 0, n_ctx, n_lat)
    gw = p_ref.shape[-1] // N_POOL_GROUPS
    R = ROUTE_ROWS

    def mix_rows(r0):
        rowi = r0 + lax.broadcasted_iota(jnp.int32, (R, 1), 0)
        pos = jnp.where(t == 0, rowi, (t - 1) * TM + rowi)
        mix = jnp.dot(ao_ref[0, r0:r0 + R, :], woa_ref[...], preferred_element_type=F32)
        for g, w in enumerate(POOL_WINDOWS):
            ls = slice(g * gw, (g + 1) * gw)
            base = r0 + HALO - w // 2
            acc = e_sc[base:base + R, ls]
            for j in range(1, w):
                acc = acc + e_sc[base + j:base + j + R, ls]
            lo = jnp.maximum(pos - w // 2, 0)
            hi = jnp.minimum(pos + w - w // 2, seq_len)
            dlt = acc / (hi - lo).astype(F32) - e_sc[r0 + HALO:r0 + HALO + R, ls]
            y = jnp.dot(dlt.astype(BF16), pw_ref[g], preferred_element_type=F32) * ps_ref[:, ls]
            mix = mix + jnp.dot(y.astype(BF16), wop_ref[ls, :], preferred_element_type=F32)
        return mix

    cnt_ref[...] = _residual_and_route(x_ref, mix_rows, mods_ref[0, 0], rw_ref, rb_ref, cnt_sc,
                                       xo_ref, hp_ref, meta_ref)


def _attn_out_call(x, mods, ao, p, pool_w, pool_scale, wo_a, wo_p, rw, rb, n_ctx):
    b, l, d = x.shape
    t = l // TM
    pool_dim = p.shape[-1]
    n_exp = rw.shape[-1] // 2
    nh = l // HALO
    per = TM // HALO
    row = lambda bb, tt: (bb, tt, 0)
    const2 = lambda bb, tt: (0, 0)
    kern = functools.partial(_attn_out_kernel, n_ctx=n_ctx, n_lat=l - n_ctx)
    return pl.pallas_call(
        kern,
        grid=(b, t),
        in_specs=[pl.BlockSpec((1, TM, d), row),
                  _mods_spec(d),
                  pl.BlockSpec((1, TM, ATTN_DIM), row),
                  pl.BlockSpec((1, TM, pool_dim), row),
                  pl.BlockSpec((1, HALO, pool_dim), lambda bb, tt: (bb, jnp.maximum(tt * per - 1, 0), 0)),
                  pl.BlockSpec((1, HALO, pool_dim), lambda bb, tt: (bb, jnp.minimum((tt + 1) * per, nh - 1), 0)),
                  pl.BlockSpec(pool_w.shape, lambda bb, tt: (0, 0, 0)),
                  pl.BlockSpec((1, pool_dim), const2),
                  pl.BlockSpec(wo_a.shape, const2),
                  pl.BlockSpec(wo_p.shape, const2),
                  pl.BlockSpec(rw.shape, lambda bb, tt: (0, 0)),
                  pl.BlockSpec((1, n_exp), const2)],
        out_specs=_route_out_specs(t, d, n_exp),
        out_shape=_route_out_shapes(b, l, d, n_exp),
        scratch_shapes=[pltpu.VMEM((TM + 2 * HALO, pool_dim), F32),
                        pltpu.VMEM((1, n_exp), F32)],
        compiler_params=_cparams(("arbitrary", "arbitrary")),
        name="attn_out",
    )(x, mods, ao, p, p, p, pool_w, pool_scale, wo_a, wo_p, rw, rb)


def _gelu_tanh(x):
    return 0.5 * x * (1.0 + jnp.tanh(0.7978845608028654 * (x + 0.044715 * (x * x * x))))


SEG_LEN = TM // SUBLANES


def _segment_perm(to_segment_major):
    ri = lax.broadcasted_iota(jnp.int32, (TM, TM), 0)
    ci = lax.broadcasted_iota(jnp.int32, (TM, TM), 1)
    if to_segment_major:
        hit = ci == (ri % SUBLANES) * SEG_LEN + ri // SUBLANES
    else:
        hit = ci == (ri % SEG_LEN) * SUBLANES + ri // SEG_LEN
    return jnp.where(hit, 1.0, 0.0).astype(BF16)


def _rg_in_body(x, mods_ref, w_ref, o_ref, gelu):
    h = _norm_mod(x, mods_ref[0, 0], 0, 1).astype(BF16)
    h = jnp.dot(_segment_perm(True), h, preferred_element_type=F32).astype(BF16)
    z = jnp.dot(h, w_ref[...], preferred_element_type=F32)
    if gelu:
        z = _gelu_tanh(z)
    o_ref[0] = z.astype(o_ref.dtype)


def _rg_in_kernel(x_ref, mods_ref, w_ref, o_ref, *, gelu):
    _rg_in_body(x_ref[0], mods_ref, w_ref, o_ref, gelu)


def _rg_in_fused_kernel(dest_ref, dest_next_ref, meta_ref, xpre_ref, modsp_ref, ys_ref,
                        mods_ref, w_ref, xo_ref, o_ref, buf_a, buf_b, sems, *, gelu):
    def consume(x):
        xo_ref[0] = x
        _rg_in_body(x, mods_ref, w_ref, o_ref, gelu)

    _moe_gather_combine(dest_ref, dest_next_ref, meta_ref, xpre_ref, modsp_ref[0, 0][5:6, :],
                        ys_ref, buf_a, buf_b, sems, consume)


def _rg_in_call(x, mods, w, gelu, out_dtype, name, pending=None):
    b, l, d = (pending.x if x is None else x).shape
    t = l // TM
    n = w.shape[1]
    row = lambda bb, tt: (bb, tt, 0)
    in_specs = [_mods_spec(d), pl.BlockSpec((d, n), lambda bb, tt: (0, 0))]
    out_spec = pl.BlockSpec((1, TM, n), row)
    out_shape = jax.ShapeDtypeStruct((b, l, n), out_dtype)
    if pending is None:
        out = pl.pallas_call(
            functools.partial(_rg_in_kernel, gelu=gelu), grid=(b, t),
            in_specs=[pl.BlockSpec((1, TM, d), row)] + in_specs,
            out_specs=out_spec, out_shape=out_shape,
            compiler_params=_cparams(("parallel", "parallel")), name=name,
        )(x, mods, w)
        return x, out
    return pl.pallas_call(
        functools.partial(_rg_in_fused_kernel, gelu=gelu), grid=(b, t),
        in_specs=_pending_specs(b, t, d) + in_specs,
        out_specs=[pl.BlockSpec((1, TM, d), row), out_spec],
        out_shape=[jax.ShapeDtypeStruct((b, l, d), F32), out_shape],
        scratch_shapes=_pending_scratch(pending.ys.shape[-1]),
        compiler_params=_cparams(("arbitrary", "arbitrary")), name="combine_" + name,
    )(*_pending_args(pending), mods, w)


def _gate_slab_starts(d_rnn, blk):
    n_slabs = -(-d_rnn // GATE_TN)
    spans = []
    for j in range(n_slabs):
        c0, c1 = j * GATE_TN, min((j + 1) * GATE_TN, d_rnn)
        lo = (c0 // blk) * blk
        hi = ((c1 - 1) // blk + 1) * blk
        spans.append((lo // LANES * LANES, -(-hi // LANES) * LANES))
    km = max(h - l for l, h in spans)
    starts = [min(l, d_rnn - km) for l, _ in spans]
    return starts, km


def _gate_slabs(r_w, i_w, d_rnn):
    blk = r_w.shape[-1]
    starts, km = _gate_slab_starts(d_rnn, blk)
    n_slabs = len(starts)
    pad = n_slabs * GATE_TN - d_rnn
    eye = jnp.eye(r_w.shape[0], dtype=r_w.dtype)
    dense = [jnp.pad((w[:, :, None, :] * eye[:, None, :, None]).reshape(d_rnn, d_rnn),
                     ((0, 0), (0, pad))) for w in (r_w, i_w)]
    slabs = [jnp.concatenate([dd[s:s + km, j * GATE_TN:(j + 1) * GATE_TN] for dd in dense], axis=1)
             for j, s in enumerate(starts)]
    return jnp.stack(slabs).astype(BF16)


def _rg_scan_kernel(u_ref, up_ref, un_ref, cw_ref, cb_ref, w_ref, gb_ref, lam_ref, h_ref,
                    e_sc, ucb_sc, uc_sc, a_sc, b_sc, carry_sc, *, reverse, starts, km):
    s = pl.program_id(1)
    nt = pl.num_programs(1)
    tau = jnp.where(s == 0, 0, nt - s) if reverse else s
    d_rnn = u_ref.shape[-1]

    @pl.when(s == 0)
    def _():
        carry_sc[...] = jnp.zeros(carry_sc.shape, F32)

    prev_ok = tau >= 2
    next_ok = (tau >= 1) & (tau <= nt - 2)
    G = SUBLANES
    sub = lax.broadcasted_iota(jnp.int32, (G, d_rnn), 0)
    for j in range(CONV_LEFT):
        src = u_ref[0, TM - (CONV_LEFT - j) * G:TM - (CONV_LEFT - j - 1) * G, :]
        edge = jnp.where(prev_ok, up_ref[0, (j + 1) * G - 1:(j + 1) * G, :], 0.0)
        e_sc[j * G:(j + 1) * G, :] = jnp.where(sub == 0, edge, pltpu.roll(src, 1, 0))
    e_sc[CONV_LEFT * G:CONV_LEFT * G + TM, :] = u_ref[0]
    edge = jnp.where(next_ok, un_ref[0, 0:1, :], 0.0)
    e_sc[CONV_LEFT * G + TM:, :] = jnp.where(sub == G - 1, edge, pltpu.roll(u_ref[0, 0:G, :], G - 1, 0))
    cw = cw_ref[...]
    n_conv_chunks = d_rnn // (3 * LANES)
    for c in range(n_conv_chunks):
        ls = slice(c * 3 * LANES, (c + 1) * 3 * LANES)
        uc = cb_ref[:, ls] + cw[0:1, ls] * e_sc[0:TM, ls]
        for j in range(1, CONV_WIDTH):
            uc = uc + cw[j:j + 1, ls] * e_sc[j * G:j * G + TM, ls]
        uc_sc[:, ls] = uc
        ucb_sc[:, ls] = uc.astype(BF16)

    lam = lam_ref[0]
    neg = -lam
    softplus = jnp.maximum(neg, 0.0) + jnp.log(1.0 + jnp.exp(-jnp.abs(neg)))
    gb = gb_ref[0]
    for j, st in enumerate(starts):
        n = min(GATE_TN, d_rnn - j * GATE_TN)
        ls = slice(j * GATE_TN, j * GATE_TN + n)
        z = jnp.dot(ucb_sc[:, st:st + km], w_ref[0, j], preferred_element_type=F32)
        r = _sigmoid(z[:, :n] + gb[0:1, ls])
        i = _sigmoid(z[:, GATE_TN:GATE_TN + n] + gb[1:2, ls])
        log_a = (-RG_C) * r * softplus[:, ls]
        a = jnp.exp(log_a)
        mult = jnp.sqrt(jnp.maximum(1.0 - a * a, 0.0))
        b_sc[:, ls] = mult * (i * uc_sc[:, ls])
        a_sc[:, ls] = a

    steps = range(SEG_LEN - 1, -1, -1) if reverse else range(SEG_LEN)
    rowi = lax.broadcasted_iota(jnp.int32, (SUBLANES, SCAN_LANES), 0)
    grp = lambda t: slice(t * SUBLANES, (t + 1) * SUBLANES)
    for c in range(d_rnn // SCAN_LANES):
        ls = slice(c * SCAN_LANES, (c + 1) * SCAN_LANES)
        tot_h = jnp.zeros((SUBLANES, SCAN_LANES), F32)
        tot_p = jnp.ones((SUBLANES, SCAN_LANES), F32)
        for t in steps:
            a = a_sc[grp(t), ls]
            tot_h = a * tot_h + b_sc[grp(t), ls]
            tot_p = a * tot_p

        for sft in (1, 2, 4):
            if reverse:
                ok = rowi < SUBLANES - sft
                shift = SUBLANES - sft
            else:
                ok = rowi >= sft
                shift = sft
            pp = jnp.where(ok, pltpu.roll(tot_p, shift, 0), 1.0)
            hp = jnp.where(ok, pltpu.roll(tot_h, shift, 0), 0.0)
            tot_h = tot_p * hp + tot_h
            tot_p = tot_p * pp
        carry = carry_sc[:, ls]
        after = tot_p * carry + tot_h
        if reverse:
            h = jnp.where(rowi < SUBLANES - 1, pltpu.roll(after, SUBLANES - 1, 0), carry)
            carry_sc[:, ls] = jnp.broadcast_to(after[0:1, :], (SUBLANES, SCAN_LANES))
        else:
            h = jnp.where(rowi >= 1, pltpu.roll(after, 1, 0), carry)
            carry_sc[:, ls] = jnp.broadcast_to(after[SUBLANES - 1:SUBLANES, :], (SUBLANES, SCAN_LANES))

        for t in steps:
            h = a_sc[grp(t), ls] * h + b_sc[grp(t), ls]
            b_sc[grp(t), ls] = h

    h_ref[0] = b_sc[...].astype(h_ref.dtype)


def _rg_scan_call(u, conv_w, conv_b, slabs, gate_b, lam, reverse, blk):
    b, l, d_rnn = u.shape
    t = l // TM
    starts, km = _gate_slab_starts(d_rnn, blk)
    lead = CONV_LEFT * SUBLANES
    per_lead = TM // lead
    per_tail = TM // SUBLANES

    def tile(ss):
        return jnp.where(ss == 0, 0, t - ss) if reverse else ss

    kern = functools.partial(_rg_scan_kernel, reverse=reverse, starts=tuple(starts), km=km)
    const2 = lambda bb, ss: (0, 0)
    return pl.pallas_call(
        kern,
        grid=(b, t),
        in_specs=[pl.BlockSpec((1, TM, d_rnn), lambda bb, ss: (bb, tile(ss), 0)),
                  pl.BlockSpec((1, lead, d_rnn),
                               lambda bb, ss: (bb, jnp.maximum(tile(ss) * per_lead - 1, 0), 0)),
                  pl.BlockSpec((1, SUBLANES, d_rnn),
                               lambda bb, ss: (bb, jnp.minimum((tile(ss) + 1) * per_tail,
                                                               l // SUBLANES - 1), 0)),
                  pl.BlockSpec(conv_w.shape, const2),
                  pl.BlockSpec((1, d_rnn), const2),
                  pl.BlockSpec((1,) + slabs.shape, lambda bb, ss: (0, 0, 0, 0)),
                  pl.BlockSpec((1, 2, d_rnn), lambda bb, ss: (0, 0, 0)),
                  pl.BlockSpec((1, 1, d_rnn), lambda bb, ss: (0, 0, 0))],
        out_specs=pl.BlockSpec((1, TM, d_rnn), lambda bb, ss: (bb, tile(ss), 0)),
        out_shape=jax.ShapeDtypeStruct((b, l, d_rnn), BF16),
        scratch_shapes=[pltpu.VMEM((TM + (CONV_WIDTH - 1) * SUBLANES, d_rnn), F32),
                        pltpu.VMEM((TM, d_rnn), BF16),
                        pltpu.VMEM((TM, d_rnn), F32),
                        pltpu.VMEM((TM, d_rnn), F32),
                        pltpu.VMEM((TM, d_rnn), F32),
                        pltpu.VMEM((SUBLANES, d_rnn), F32)],
        compiler_params=_cparams(("arbitrary", "arbitrary")),
        name="rg_scan_bwd" if reverse else "rg_scan_fwd",
    )(u, u, u, conv_w, conv_b, slabs[None], gate_b[None], lam[None, None])


def _rg_out_kernel(x_ref, mods_ref, g_ref, hf_ref, hb_ref, wo_ref, rw_ref, rb_ref,
                   xo_ref, hp_ref, meta_ref, cnt_ref, cnt_sc):
    gated = g_ref[0].astype(F32) * (hf_ref[0].astype(F32) + hb_ref[0].astype(F32))
    perm = _segment_perm(False)
    gated = gated.astype(BF16)

    def mix_rows(r0):
        rows = jnp.dot(perm[r0:r0 + ROUTE_ROWS, :], gated, preferred_element_type=F32)
        return jnp.dot(rows.astype(BF16), wo_ref[...], preferred_element_type=F32)

    cnt_ref[...] = _residual_and_route(x_ref, mix_rows, mods_ref[0, 0], rw_ref, rb_ref, cnt_sc,
                                       xo_ref, hp_ref, meta_ref)


def _rg_out_call(x, mods, gg, hf, hb, w_out, rw, rb):
    b, l, d = x.shape
    t = l // TM
    d_rnn = gg.shape[-1]
    n_exp = rw.shape[-1] // 2
    row = lambda bb, tt: (bb, tt, 0)
    const2 = lambda bb, tt: (0, 0)
    return pl.pallas_call(
        _rg_out_kernel,
        grid=(b, t),
        in_specs=[pl.BlockSpec((1, TM, d), row), _mods_spec(d),
                  pl.BlockSpec((1, TM, d_rnn), row),
                  pl.BlockSpec((1, TM, d_rnn), row),
                  pl.BlockSpec((1, TM, d_rnn), row),
                  pl.BlockSpec(w_out.shape, const2),
                  pl.BlockSpec(rw.shape, lambda bb, tt: (0, 0)),
                  pl.BlockSpec((1, n_exp), const2)],
        out_specs=_route_out_specs(t, d, n_exp),
        out_shape=_route_out_shapes(b, l, d, n_exp),
        scratch_shapes=[pltpu.VMEM((1, n_exp), F32)],
        compiler_params=_cparams(("arbitrary", "arbitrary")),
        name="rg_out",
    )(x, mods, gg, hf, hb, w_out, rw, rb)


GU_ROWS = 512
EXPERT_ROWS = 256


def _gu_split_body(w_ref, o_ref):
    de = w_ref.shape[-1] // 2
    pair = 2 * LANES
    ri = lax.broadcasted_iota(jnp.int32, (pair, pair), 0)
    ci = lax.broadcasted_iota(jnp.int32, (pair, pair), 1)
    src = jnp.where(ci < LANES, 2 * ci, 2 * (ci - LANES) + 1)
    sel = jnp.where(ri == src, 1.0, 0.0).astype(BF16)
    for k in range(w_ref.shape[1]):
        w = w_ref[0, k].astype(BF16)
        for j in range(w.shape[1] // pair):
            r = jnp.dot(w[:, j * pair:(j + 1) * pair], sel, preferred_element_type=F32)
            o_ref[k, :, j * LANES:(j + 1) * LANES] = r[:, :LANES].astype(BF16)
            o_ref[k, :, de + j * LANES:de + (j + 1) * LANES] = r[:, LANES:].astype(BF16)


def _dispatch_kernel(pend_ref, padded_ref, dest_ref, hp_ref, w_ref, xs_ref, wgu_ref,
                     zero_sc, sem, zsem, *, n_blocks):
    step = pl.program_id(0)
    n_exp = pend_ref.shape[0]

    @pl.when(step == 0)
    def _():
        zero_sc[...] = jnp.zeros(zero_sc.shape, U32)

        def zero_block(row0):
            rows = pl.ds(pl.multiple_of(row0, MOE_BLK), MOE_BLK)
            return pltpu.make_async_copy(zero_sc, xs_ref.at[rows], zsem)

        first_tail = pend_ref[n_exp - 1] // MOE_BLK
        for go in (lambda cp: cp.start(), lambda cp: cp.wait()):
            def per_expert(e, carry, go=go):
                @pl.when(padded_ref[e] > 0)
                def _():
                    go(zero_block(pend_ref[e] - MOE_BLK))
                return carry
            lax.fori_loop(0, n_exp, per_expert, 0)

            def per_tail(i, carry, go=go):
                go(zero_block(i * MOE_BLK))
                return carry
            lax.fori_loop(first_tail, n_blocks, per_tail, 0)

    for r in range(TM):
        for kk in range(TOP_K):
            d = dest_ref[r * TOP_K + kk]
            pltpu.make_async_copy(hp_ref.at[pl.ds(r, 1)], xs_ref.at[pl.ds(d, 1)], sem).start(
                priority=kk % 2)
    _gu_split_body(w_ref, wgu_ref)
    for _ in range(TOP_K):
        pltpu.make_async_copy(hp_ref, xs_ref.at[pl.ds(0, TM)], sem).wait()


def _dispatch_call(pend, padded, dest_flat, hp, n_blocks, w_gu_all, layer):
    n, c = hp.shape
    n_layers, n_exp, d, de2 = w_gu_all.shape
    n_wblk = n_exp * (d // GU_ROWS)
    kb = next(k for k in range(1, n_wblk + 1) if n_wblk % k == 0 and n_wblk // k <= n // TM)
    n_wsteps = n_wblk // kb

    def wstep(i):
        return jnp.minimum(i, n_wsteps - 1)

    kern = functools.partial(_dispatch_kernel, n_blocks=n_blocks)
    xs, wgu = pl.pallas_call(
        kern,
        grid_spec=pltpu.PrefetchScalarGridSpec(
            num_scalar_prefetch=2,
            grid=(n // TM,),
            in_specs=[pl.BlockSpec((TM * TOP_K,), lambda i, *_: (i,), memory_space=pltpu.SMEM),
                      pl.BlockSpec((TM, c), lambda i, *_: (i, 0)),
                      pl.BlockSpec((1, kb, GU_ROWS, de2), lambda i, *_: (layer, wstep(i), 0, 0))],
            out_specs=[pl.BlockSpec(memory_space=pl.ANY),
                       pl.BlockSpec((kb, GU_ROWS, de2), lambda i, *_: (wstep(i), 0, 0))],
            scratch_shapes=[pltpu.VMEM((MOE_BLK, c), U32),
                            pltpu.SemaphoreType.DMA(()),
                            pltpu.SemaphoreType.DMA(())]),
        out_shape=[jax.ShapeDtypeStruct((n_blocks * MOE_BLK, c), U32),
                   jax.ShapeDtypeStruct((n_wblk, GU_ROWS, de2), BF16)],
        compiler_params=pltpu.CompilerParams(dimension_semantics=("arbitrary",),
                                             vmem_limit_bytes=VMEM_LIMIT, has_side_effects=True),
        name="moe_dispatch",
    )(pend, padded, dest_flat, hp, w_gu_all.reshape(n_layers, n_wblk, GU_ROWS, de2))
    return xs, wgu


def _expert_kernel(blk_e_ref, n_used_ref, xs_ref, wgu_ref, bgu_ref, wd_ref, bd_ref, ys_ref):
    used = pl.program_id(0) < n_used_ref[0]

    @pl.when(used)
    def _():
        for r0 in range(0, MOE_BLK, EXPERT_ROWS):
            rows = slice(r0, r0 + EXPERT_ROWS)
            w = xs_ref[rows, :]
            x = jnp.concatenate([_unpack_hi(w).astype(BF16), _unpack_lo(w).astype(BF16)], axis=-1)
            gu = bgu_ref[0]
            for k in range(wgu_ref.shape[0]):
                gu = gu + jnp.dot(x[:, k * GU_ROWS:(k + 1) * GU_ROWS], wgu_ref[k],
                                  preferred_element_type=F32)
            de = gu.shape[1] // 2
            g = jnp.minimum(gu[:, :de], SWIGLU_LIMIT)
            u = jnp.clip(gu[:, de:], -SWIGLU_LIMIT, SWIGLU_LIMIT)
            act = (u + 1.0) * (g * _sigmoid(SWIGLU_ALPHA * g))
            y = jnp.dot(act.astype(BF16), wd_ref[0], preferred_element_type=F32) + bd_ref[0]
            ys_ref[rows, :] = _pack_bf16_pair(y)

    @pl.when(jnp.logical_not(used))
    def _():
        ys_ref[...] = jnp.zeros(ys_ref.shape, U32)


def _expert_call(blk_e, n_used, xs, wgu, bgu, wd, bd):
    p_rows, c = xs.shape
    n_exp, de, d = wd.shape
    de2 = 2 * de
    per_e = d // GU_ROWS
    by_e3 = lambda i, be, nu: (be[i], 0, 0)
    return pl.pallas_call(
        _expert_kernel,
        grid_spec=pltpu.PrefetchScalarGridSpec(
            num_scalar_prefetch=2,
            grid=(p_rows // MOE_BLK,),
            in_specs=[pl.BlockSpec((MOE_BLK, c), lambda i, be, nu: (i, 0)),
                      pl.BlockSpec((per_e, GU_ROWS, de2), by_e3),
                      pl.BlockSpec((1, 1, de2), by_e3),
                      pl.BlockSpec((1, de, d), by_e3),
                      pl.BlockSpec((1, 1, d), by_e3)],
            out_specs=pl.BlockSpec((MOE_BLK, c), lambda i, be, nu: (i, 0))),
        out_shape=jax.ShapeDtypeStruct((p_rows, c), U32),
        compiler_params=_cparams(("arbitrary",)),
        name="moe_experts",
    )(blk_e, n_used, xs, wgu, bgu, wd, bd)


def _moe_gather_combine(dest_ref, dest_next_ref, meta_ref, x_ref, gf, ys_ref, buf_a, buf_b, sems,
                        consume):
    step = pl.program_id(0) * pl.num_programs(1) + pl.program_id(1)
    n_steps = pl.num_programs(0) * pl.num_programs(1)
    c = buf_a.shape[-1]

    def request(idx_ref, buf, s):
        for r in range(TM):
            for kk in range(TOP_K):
                d = idx_ref[r * TOP_K + kk]
                pltpu.make_async_copy(ys_ref.at[pl.ds(d, 1)], buf.at[kk, pl.ds(r, 1)],
                                      sems.at[s]).start(priority=kk % 2)

    def wait(buf, s):
        for kk in range(TOP_K):
            pltpu.make_async_copy(ys_ref.at[pl.ds(0, TM)], buf.at[kk], sems.at[s]).wait()

    @pl.when(step == 0)
    def _():
        request(dest_ref, buf_a, 0)

    def run(cur, s_cur, nxt, s_nxt):
        wait(cur, s_cur)
        request(dest_next_ref, nxt, s_nxt)
        meta = meta_ref[...]
        y_hi = jnp.zeros((TM, c), F32)
        y_lo = jnp.zeros((TM, c), F32)
        for kk in range(TOP_K):
            gk = meta[:, TOP_K + kk:TOP_K + kk + 1]
            w = cur[kk]
            y_hi = y_hi + gk * _unpack_hi(w)
            y_lo = y_lo + gk * _unpack_lo(w)
        consume(jnp.concatenate([x_ref[0, :, :c] + gf[:, :c] * y_hi,
                                 x_ref[0, :, c:] + gf[:, c:] * y_lo], axis=-1))

        @pl.when(step == n_steps - 1)
        def _():
            wait(nxt, s_nxt)

    @pl.when(step % 2 == 0)
    def _():
        run(buf_a, 0, buf_b, 1)

    @pl.when(step % 2 == 1)
    def _():
        run(buf_b, 1, buf_a, 0)


class _PendingCombine(NamedTuple):
    dest: jax.Array
    meta: jax.Array
    x: jax.Array
    mods: jax.Array
    ys: jax.Array


def _pending_specs(b, t, d):
    return [pl.BlockSpec((TM * TOP_K,), lambda bb, tt: (bb * t + tt,), memory_space=pltpu.SMEM),
            pl.BlockSpec((TM * TOP_K,), lambda bb, tt: (jnp.minimum(bb * t + tt + 1, b * t - 1),),
                         memory_space=pltpu.SMEM),
            pl.BlockSpec((TM, LANES), lambda bb, tt: (bb * t + tt, 0)),
            pl.BlockSpec((1, TM, d), lambda bb, tt: (bb, tt, 0)),
            _mods_spec(d),
            pl.BlockSpec(memory_space=pl.ANY)]


def _pending_scratch(c):
    return [pltpu.VMEM((TOP_K, TM, c), U32), pltpu.VMEM((TOP_K, TM, c), U32),
            pltpu.SemaphoreType.DMA((2,))]


def _pending_args(p):
    return (p.dest, p.dest, p.meta, p.x, p.mods, p.ys)


def _combine_kernel(dest_ref, dest_next_ref, meta_ref, x_ref, mods_ref, ys_ref, o_ref,
                    buf_a, buf_b, sems):
    def consume(x):
        o_ref[0] = x

    _moe_gather_combine(dest_ref, dest_next_ref, meta_ref, x_ref, mods_ref[0, 0][5:6, :],
                        ys_ref, buf_a, buf_b, sems, consume)


def _combine_call(pending):
    b, l, d = pending.x.shape
    t = l // TM
    return pl.pallas_call(
        _combine_kernel,
        grid=(b, t),
        in_specs=_pending_specs(b, t, d),
        out_specs=pl.BlockSpec((1, TM, d), lambda bb, tt: (bb, tt, 0)),
        out_shape=jax.ShapeDtypeStruct((b, l, d), F32),
        scratch_shapes=_pending_scratch(pending.ys.shape[-1]),
        compiler_params=_cparams(("arbitrary", "arbitrary")),
        name="moe_combine",
    )(*_pending_args(pending))


def _moe(x_new, hp, meta, counts, mods, w_gu_all, layer, bgu, wd, bd):
    n = hp.shape[0]
    n_exp = w_gu_all.shape[1]
    n_blocks = -(-n * TOP_K // MOE_BLK) + n_exp
    counts = counts.reshape(n_exp).astype(jnp.int32)
    padded = (counts + MOE_BLK - 1) // MOE_BLK * MOE_BLK
    pend = jnp.cumsum(padded).astype(jnp.int32)
    pstart = pend - padded
    top_e = meta[:, 0:TOP_K].astype(jnp.int32)
    rank = meta[:, 2 * TOP_K:3 * TOP_K].astype(jnp.int32)
    dest = (pstart[top_e] + rank).reshape(n * TOP_K)
    blk_row0 = jnp.arange(n_blocks, dtype=jnp.int32) * MOE_BLK
    blk_e = jnp.minimum(jnp.sum((pend[None, :] <= blk_row0[:, None]).astype(jnp.int32), axis=1),
                        n_exp - 1)
    xs, wgu = _dispatch_call(pend, padded, dest, hp, n_blocks, w_gu_all, layer)
    ys = _expert_call(blk_e, pend[n_exp - 1:] // MOE_BLK, xs, wgu, bgu, wd, bd)
    return _PendingCombine(dest, meta, x_new, mods, ys)


def _final_norm_kernel(x_ref, g_ref, o_ref):
    x = x_ref[0]
    o_ref[0] = x * lax.rsqrt(jnp.mean(x * x, axis=-1, keepdims=True) + EPS) * g_ref[...]


def _final_norm_call(x, gain, n_ctx):
    b, l, d = x.shape
    s = l - n_ctx
    skip = n_ctx // TM
    return pl.pallas_call(
        _final_norm_kernel,
        grid=(b, s // TM),
        in_specs=[pl.BlockSpec((1, TM, d), lambda bb, tt: (bb, tt + skip, 0)),
                  pl.BlockSpec((1, d), lambda bb, tt: (0, 0))],
        out_specs=pl.BlockSpec((1, TM, d), lambda bb, tt: (bb, tt, 0)),
        out_shape=jax.ShapeDtypeStruct((b, s, d), F32),
        compiler_params=_cparams(("parallel", "parallel")),
        name="final_norm",
    )(x, gain.reshape(1, d))


def _rope_tables(s, n_ctx):
    t = jnp.arange(s, dtype=jnp.int32)
    inv_freq = ROPE_THETA ** (-jnp.arange(ROPE_AXIS_DIM // 2, dtype=F32) * 2.0 / ROPE_AXIS_DIM)
    ang_r = (t // GRID_W).astype(F32)[:, None] * inv_freq
    ang_c = (t % GRID_W).astype(F32)[:, None] * inv_freq
    cr, sr, cc, sc = jnp.cos(ang_r), jnp.sin(ang_r), jnp.cos(ang_c), jnp.sin(ang_c)
    cos_t = jnp.concatenate([cr, cr, cc, cc], axis=-1)
    sin_t = jnp.concatenate([-sr, sr, -sc, sc], axis=-1)
    cos_t = jnp.concatenate([jnp.ones((n_ctx, HEAD_DIM), F32), cos_t], axis=0)
    sin_t = jnp.concatenate([jnp.zeros((n_ctx, HEAD_DIM), F32), sin_t], axis=0)
    return cos_t, sin_t


def kernel(x, c, ctx, c_ctx, mod_w, mod_b, ap_w_in, ap_q_gain, ap_k_gain, ap_pool_w, ap_pool_scale, ap_w_out, rg_w_in, rg_conv_w, rg_conv_b, rg_gate_r_w, rg_gate_r_b, rg_gate_i_w, rg_gate_i_b, rg_lambda, rg_w_out, moe_router_w, moe_router_b, moe_w_gu, moe_b_gu, moe_w_down, moe_b_down, final_norm_g):
    b, s, d = x.shape
    n_ctx = ctx.shape[1]
    depth = mod_w.shape[0]
    assert n_ctx % TM == 0 and s % TM == 0 and s % (2 * ATT_CK) == 0 and s % GRID_W == 0
    assert b + 1 <= SUBLANES and d % (2 * LANES) == 0

    cv = jnp.zeros((SUBLANES, d), F32).at[:b].set(c).at[b].set(c_ctx)
    mods_all = _mods_call(cv, mod_w, mod_b).reshape(depth, SUBLANES, 6, d)
    cos_t, sin_t = _rope_tables(s, n_ctx)

    xx = jnp.concatenate([ctx, x], axis=1)
    pending = None
    d_rnn = rg_w_out.shape[1]
    rnn_blk = rg_gate_r_w.shape[-1]
    assert d_rnn % SCAN_LANES == 0 and d_rnn % (3 * LANES) == 0

    for l in range(depth):
        i = l // 2
        ml = mods_all[l]
        mods = jnp.stack([jnp.broadcast_to(ml[b][None], (b, 6, d)), ml[:b]])
        rw_hi = moe_router_w[l].astype(BF16)
        rw = jnp.concatenate([rw_hi, (moe_router_w[l] - rw_hi.astype(F32)).astype(BF16)],
                             axis=1)
        rb = moe_router_b[l].reshape(1, -1)
        if l % 2 == 0:
            if pending is not None:
                xx = _combine_call(pending)
            q, k, v, p = _attn_in_call(xx, mods, ap_w_in[i].astype(BF16),
                                       ap_q_gain[i].reshape(1, -1), ap_k_gain[i].reshape(1, -1),
                                       cos_t, sin_t)
            ao = _attention_call(q, k, v, n_ctx)
            wo = ap_w_out[i].astype(BF16)
            x_new, hp, meta, counts = _attn_out_call(
                xx, mods, ao, p, ap_pool_w[i].astype(BF16), ap_pool_scale[i].reshape(1, -1),
                wo[:ATTN_DIM], wo[ATTN_DIM:], rw, rb, n_ctx)
        else:
            w_in = rg_w_in[i].astype(BF16)
            xx, gg = _rg_in_call(xx, mods, w_in[:, :d_rnn], True, BF16, "rg_in_gate", pending)
            _, u = _rg_in_call(xx, mods, w_in[:, d_rnn:], False, F32, "rg_in_u")
            hs = []
            for dr in range(2):
                slabs = _gate_slabs(rg_gate_r_w[i, dr], rg_gate_i_w[i, dr], d_rnn)
                gate_b = jnp.stack([rg_gate_r_b[i, dr], rg_gate_i_b[i, dr]])
                hs.append(_rg_scan_call(u, rg_conv_w[i], rg_conv_b[i].reshape(1, -1), slabs, gate_b,
                                        rg_lambda[i, dr], dr == 1, rnn_blk))
            x_new, hp, meta, counts = _rg_out_call(xx, mods, gg, hs[0], hs[1],
                                                   rg_w_out[i].astype(BF16), rw, rb)
        de = moe_w_down.shape[2]
        bgu = jnp.concatenate([moe_b_gu[l][:, 0::2], moe_b_gu[l][:, 1::2]], axis=-1)
        pending = _moe(x_new, hp, meta, counts, mods, moe_w_gu, l, bgu.reshape(-1, 1, 2 * de),
                       moe_w_down[l].astype(BF16), moe_b_down[l].reshape(-1, 1, d))
        xx = None

    return _final_norm_call(_combine_call(pending), final_norm_g, n_ctx)
```
